```python
import math
import jax, jax.numpy as jnp
from jax import lax
import numpy as np

D_MODEL = 2048
BATCH = 1
SEQ = 8192
DEPTH = 1

GMLP_CHUNK = 128
GMLP_GROUPS = 4
GMLP_GROUP_DIM = 128
GMLP_WIDTH = GMLP_GROUPS * GMLP_GROUP_DIM

DN_HEADS = 8
DN_HEAD_DIM = 128
DN_WIDTH = DN_HEADS * DN_HEAD_DIM
DN_CONV = 4
DN_CHUNK = 64

XA_HEADS = 4
XA_HEAD_DIM = 128
XA_WIDTH = XA_HEADS * XA_HEAD_DIM
N_MEM = 256

MIX_WIDTH = GMLP_WIDTH + DN_WIDTH + XA_WIDTH
EPS = 1e-6

SEG_WIDTHS = (GMLP_WIDTH, GMLP_WIDTH, GMLP_WIDTH,
              DN_WIDTH, DN_WIDTH, DN_WIDTH, DN_WIDTH, DN_HEADS, DN_HEADS,
              XA_WIDTH, XA_WIDTH)
IN_WIDTH = sum(SEG_WIDTHS)
SPLIT_POINTS = tuple(sum(SEG_WIDTHS[:i + 1]) for i in range(len(SEG_WIDTHS) - 1))

kernel_name = "hybrid_gmlp_gated_deltanet_memxattn_block"


def rmsnorm(x, g):
    xf = x.astype(jnp.float32)
    y = xf * lax.rsqrt(jnp.mean(xf * xf, axis=-1, keepdims=True) + EPS)
    return (y * g.astype(jnp.float32)).astype(x.dtype)


def layernorm(x, g, b):
    xf = x.astype(jnp.float32)
    mu = jnp.mean(xf, axis=-1, keepdims=True)
    xc = xf - mu
    y = xc * lax.rsqrt(jnp.mean(xc * xc, axis=-1, keepdims=True) + EPS)
    return (y * g.astype(jnp.float32) + b.astype(jnp.float32)).astype(x.dtype)


def l2norm(x):
    xf = x.astype(jnp.float32)
    return xf * lax.rsqrt(jnp.sum(xf * xf, axis=-1, keepdims=True) + EPS)


def gmlp_spatial_gating(u, v, ws, bs, ln_g, ln_b):
    B, S, _ = u.shape
    u = jax.nn.gelu(u)
    v = layernorm(jax.nn.gelu(v), ln_g, ln_b)
    nc = S // GMLP_CHUNK
    v = v.reshape(B, nc, GMLP_CHUNK, GMLP_GROUPS, GMLP_GROUP_DIM)
    causal = jnp.tril(jnp.ones((GMLP_CHUNK, GMLP_CHUNK), dtype=bool))
    w = jnp.where(causal[None], ws, 0)
    s = jnp.einsum('gts,bcsgd->bctgd', w, v) + bs.T[None, None, :, :, None]
    return u * s.reshape(B, S, GMLP_WIDTH)


def causal_conv_silu(x, w):
    K = w.shape[0]
    S = x.shape[1]
    xp = jnp.pad(x, ((0, 0), (K - 1, 0), (0, 0)))
    y = sum(w[k] * xp[:, k:k + S] for k in range(K))
    return jax.nn.silu(y)


def gated_delta_rule(q, k, v, g, beta):
    f32 = jnp.float32
    q, k, v, g, beta = (t.astype(f32) for t in (q, k, v, g, beta))
    B, H, S, Dk = q.shape
    Dv = v.shape[-1]
    C = DN_CHUNK
    N = S // C
    q = q.reshape(B, H, N, C, Dk)
    k = k.reshape(B, H, N, C, Dk)
    v = v.reshape(B, H, N, C, Dv)
    g = g.reshape(B, H, N, C)
    beta = beta.reshape(B, H, N, C)

    g_cum = jnp.cumsum(g, axis=-1)
    tri = jnp.tril(jnp.ones((C, C), dtype=bool))
    strict = jnp.tril(jnp.ones((C, C), dtype=bool), k=-1)
    diff = g_cum[..., :, None] - g_cum[..., None, :]
    decay = jnp.exp(jnp.where(tri, diff, -jnp.inf))

    kb = k * beta[..., None]
    vb = v * beta[..., None]
    a = jnp.where(strict, jnp.einsum('bhnid,bhnjd->bhnij', kb, k) * decay, 0.0)
    eye = jnp.eye(C, dtype=f32)
    T = lax.linalg.triangular_solve(a + eye, jnp.broadcast_to(eye, a.shape),
                                    left_side=True, lower=True, unit_diagonal=True)
    value = jnp.einsum('bhnij,bhnjd->bhnid', T, vb)
    k_cumdecay = jnp.einsum('bhnij,bhnjd->bhnid', T, kb * jnp.exp(g_cum)[..., None])
    attn_intra = jnp.where(tri, jnp.einsum('bhnid,bhnjd->bhnij', q, k) * decay, 0.0)
    q_g = q * jnp.exp(g_cum)[..., None]
    g_last = g_cum[..., -1]
    k_dec = k * jnp.exp(g_last[..., None] - g_cum)[..., None]

    def step(state, inp):
        qg_c, kcd_c, val_c, ai_c, kd_c, gl_c = inp
        v_new = val_c - jnp.einsum('bhid,bhde->bhie', kcd_c, state)
        o_c = jnp.einsum('bhid,bhde->bhie', qg_c, state) + jnp.einsum('bhij,bhje->bhie', ai_c, v_new)
        state = state * jnp.exp(gl_c)[..., None, None] + jnp.einsum('bhid,bhie->bhde', kd_c, v_new)
        return state, o_c

    xs = tuple(jnp.moveaxis(t, 2, 0) for t in (q_g, k_cumdecay, value, attn_intra, k_dec, g_last))
    s0 = jnp.zeros((B, H, Dk, Dv), f32)
    _, o = lax.scan(step, s0, xs)
    return jnp.moveaxis(o, 0, 2).reshape(B, H, S, Dv)


def setup_inputs(seed: int = 0) -> dict:
    key = jax.random.key(seed)
    ks = jax.random.split(key, 16)
    f32 = jnp.float32
    nrm = lambda k_, shp: jax.random.normal(k_, shp, f32)
    x = nrm(ks[0], (BATCH, SEQ, D_MODEL))
    mem = nrm(ks[1], (BATCH, N_MEM, D_MODEL))
    ln_g = 1.0 + 0.1 * nrm(ks[2], (DEPTH, D_MODEL))
    w_in = nrm(ks[3], (DEPTH, D_MODEL, IN_WIDTH)) * D_MODEL ** -0.5
    gmlp_ln_g = 1.0 + 0.1 * nrm(ks[4], (DEPTH, GMLP_WIDTH))
    gmlp_ln_b = 0.1 * nrm(ks[5], (DEPTH, GMLP_WIDTH))
    gmlp_ws = nrm(ks[6], (DEPTH, GMLP_GROUPS, GMLP_CHUNK, GMLP_CHUNK)) * GMLP_CHUNK ** -0.5
    gmlp_bs = 1.0 + 0.1 * nrm(ks[7], (DEPTH, GMLP_GROUPS, GMLP_CHUNK))
    conv_w = nrm(ks[8], (DEPTH, DN_CONV, 3 * DN_WIDTH)) * DN_CONV ** -0.5
    dn_a_log = jnp.log(jax.random.uniform(ks[9], (DEPTH, DN_HEADS), f32, minval=1.0, maxval=16.0))
    dt = jnp.exp(jax.random.uniform(ks[10], (DEPTH, DN_HEADS), f32,
                                    minval=math.log(1e-3), maxval=math.log(1e-1)))
    dn_dt_bias = dt + jnp.log(-jnp.expm1(-dt))
    dn_norm_g = 1.0 + 0.1 * nrm(ks[11], (DEPTH, DN_HEAD_DIM))
    mem_norm_g = 1.0 + 0.1 * nrm(ks[12], (DEPTH, D_MODEL))
    w_mem_kv = nrm(ks[13], (DEPTH, D_MODEL, 2 * XA_WIDTH)) * D_MODEL ** -0.5
    w_out = nrm(ks[14], (DEPTH, MIX_WIDTH, D_MODEL)) * MIX_WIDTH ** -0.5
    final_g = 1.0 + 0.1 * nrm(ks[15], (D_MODEL,))
    return {"x": x, "mem": mem, "ln_g": ln_g, "w_in": w_in,
            "gmlp_ln_g": gmlp_ln_g, "gmlp_ln_b": gmlp_ln_b, "gmlp_ws": gmlp_ws, "gmlp_bs": gmlp_bs,
            "conv_w": conv_w, "dn_a_log": dn_a_log, "dn_dt_bias": dn_dt_bias, "dn_norm_g": dn_norm_g,
            "mem_norm_g": mem_norm_g, "w_mem_kv": w_mem_kv, "w_out": w_out, "final_g": final_g}


def reference(x, mem, ln_g, w_in, gmlp_ln_g, gmlp_ln_b, gmlp_ws, gmlp_bs, conv_w, dn_a_log,
              dn_dt_bias, dn_norm_g, mem_norm_g, w_mem_kv, w_out, final_g):
    B, S, _ = x.shape
    M = mem.shape[1]
    for l in range(DEPTH):
        h = rmsnorm(x, ln_g[l])
        proj = h @ w_in[l]
        (g_u, g_v, g_z, d_q, d_k, d_v, d_z, d_a, d_b, c_q, c_z) = jnp.split(proj, SPLIT_POINTS, axis=-1)

        out_a = gmlp_spatial_gating(g_u, g_v, gmlp_ws[l], gmlp_bs[l], gmlp_ln_g[l], gmlp_ln_b[l]) * jax.nn.silu(g_z)

        qkv = causal_conv_silu(jnp.concatenate([d_q, d_k, d_v], axis=-1), conv_w[l])
        q, k, v = jnp.split(qkv, 3, axis=-1)
        to_heads = lambda t: t.reshape(B, S, DN_HEADS, DN_HEAD_DIM).transpose(0, 2, 1, 3)
        q = l2norm(to_heads(q)) * DN_HEAD_DIM ** -0.5
        k = l2norm(to_heads(k))
        v = to_heads(v)
        g = -jnp.exp(dn_a_log[l].astype(jnp.float32)) * jax.nn.softplus(
            d_a.astype(jnp.float32) + dn_dt_bias[l].astype(jnp.float32))
        beta = jax.nn.sigmoid(d_b.astype(jnp.float32))
        o = gated_delta_rule(q, k, v, g.transpose(0, 2, 1), beta.transpose(0, 2, 1))
        o = o.transpose(0, 2, 1, 3).astype(x.dtype)
        o = rmsnorm(o, dn_norm_g[l]) * jax.nn.silu(d_z.reshape(B, S, DN_HEADS, DN_HEAD_DIM))
        out_b = o.reshape(B, S, DN_WIDTH)

        m = rmsnorm(mem, mem_norm_g[l])
        mk, mv = jnp.split(m @ w_mem_kv[l], 2, axis=-1)
        mk = mk.reshape(B, M, XA_HEADS, XA_HEAD_DIM)
        mv = mv.reshape(B, M, XA_HEADS, XA_HEAD_DIM)
        cq = c_q.reshape(B, S, XA_HEADS, XA_HEAD_DIM)
        scores = jnp.einsum('bshd,bmhd->bhsm', cq, mk).astype(jnp.float32) * XA_HEAD_DIM ** -0.5
        p = jax.nn.softmax(scores, axis=-1).astype(mv.dtype)
        out_c = jnp.einsum('bhsm,bmhd->bshd', p, mv).reshape(B, S, XA_WIDTH) * jax.nn.silu(c_z)

        mixed = jnp.concatenate([out_a, out_b, out_c], axis=-1)
        x = x + mixed @ w_out[l]
    return rmsnorm(x, final_g)
```

```python
import functools

import jax
import jax.numpy as jnp
from jax import lax
from jax.experimental import pallas as pl
from jax.experimental.pallas import tpu as pltpu

F32 = jnp.float32
BF16 = jnp.bfloat16

D_MODEL = 2048
GMLP_CHUNK = 128
GMLP_GROUPS = 4
GMLP_WIDTH = 512
DN_HEADS = 8
DN_HEAD_DIM = 128
DN_WIDTH = 1024
DN_CONV = 4
XA_HEADS = 4
XA_HEAD_DIM = 128
XA_WIDTH = 512
EPS = 1e-6

LANES = 128
SUBLANES = 8
DN_BLOCK = 128
INV_BASE = 16

MAIN_WIDTH = 3 * GMLP_WIDTH + 4 * DN_WIDTH + 2 * XA_WIDTH
PROJ_TN = 512
COL_Q, COL_K, COL_DV, COL_DZ = 0, 2, 4, 6
COL_U, COL_V, COL_GZ = 8, 9, 10
COL_CQ, COL_CZ = 11, 12

VMEM_LIMIT = 48 * 1024 * 1024


def _dot(a, b):
    return jnp.dot(a, b, preferred_element_type=F32)


def _silu(x):
    return x * jax.nn.sigmoid(x)


def _proj_kernel(x_ref, g_ref, w_ref, wab_ref, p_ref, ab_ref, h_ref, *, row_chunk):
    n = pl.program_id(1)
    tm = x_ref.shape[0]

    @pl.when(n == 0)
    def _():
        def body(c, carry):
            r0 = pl.multiple_of(c * row_chunk, row_chunk)
            xf = x_ref[pl.ds(r0, row_chunk), :]
            ms = jnp.mean(xf * xf, axis=-1, keepdims=True)
            h_ref[pl.ds(r0, row_chunk), :] = (xf * lax.rsqrt(ms + EPS) * g_ref[...]).astype(BF16)
            return carry

        lax.fori_loop(0, tm // row_chunk, body, 0)
        ab_ref[...] = _dot(h_ref[...], wab_ref[...])

    p_ref[...] = _dot(h_ref[...], w_ref[...]).astype(BF16)


def _proj(x2, ln_g, w_main, w_ab):
    S = x2.shape[0]
    tm = min(1024, S)
    grid = (S // tm, MAIN_WIDTH // PROJ_TN)
    return pl.pallas_call(
        functools.partial(_proj_kernel, row_chunk=128),
        out_shape=(jax.ShapeDtypeStruct((S, MAIN_WIDTH), BF16),
                   jax.ShapeDtypeStruct((S, LANES), F32)),
        grid=grid,
        in_specs=[
            pl.BlockSpec((tm, D_MODEL), lambda m, n: (m, 0)),
            pl.BlockSpec((1, D_MODEL), lambda m, n: (0, 0)),
            pl.BlockSpec((D_MODEL, PROJ_TN), lambda m, n: (0, n)),
            pl.BlockSpec((D_MODEL, LANES), lambda m, n: (0, 0)),
        ],
        out_specs=(
            pl.BlockSpec((tm, PROJ_TN), lambda m, n: (m, n)),
            pl.BlockSpec((tm, LANES), lambda m, n: (m, 0)),
        ),
        scratch_shapes=[pltpu.VMEM((tm, D_MODEL), BF16)],
        compiler_params=pltpu.CompilerParams(
            dimension_semantics=("arbitrary", "arbitrary"),
            vmem_limit_bytes=VMEM_LIMIT),
        name="proj",
    )(x2, ln_g, w_main, w_ab)


def _gmlp_kernel(u_ref, v_ref, z_ref, lng_ref, lnb_ref, ws_ref, bst_ref, o_ref):
    tm = u_ref.shape[0]
    T = GMLP_CHUNK
    v = jax.nn.gelu(v_ref[...].astype(F32))
    mu = jnp.mean(v, axis=-1, keepdims=True)
    vc = v - mu
    var = jnp.mean(vc * vc, axis=-1, keepdims=True)
    vn = (vc * lax.rsqrt(var + EPS) * lng_ref[...] + lnb_ref[...]).astype(BF16)
    row = lax.broadcasted_iota(jnp.int32, (T, T), 0)
    col = lax.broadcasted_iota(jnp.int32, (T, T), 1)
    causal = row >= col
    for g in range(GMLP_GROUPS):
        w = jnp.where(causal, ws_ref[g], 0.0).astype(BF16)
        bias = bst_ref[:, g:g + 1]
        cs = slice(g * LANES, (g + 1) * LANES)
        for c in range(tm // T):
            rs = slice(c * T, (c + 1) * T)
            s = _dot(w, vn[rs, cs]) + bias
            u = jax.nn.gelu(u_ref[rs, cs].astype(F32))
            o_ref[rs, cs] = (u * s * _silu(z_ref[rs, cs].astype(F32))).astype(BF16)


def _gmlp(p, ln_g, ln_b, ws, bs_t):
    S = p.shape[0]
    tm = min(512, S)
    blk = lambda j: pl.BlockSpec((tm, GMLP_WIDTH), lambda m, j=j: (m, j))
    full = lambda shp: pl.BlockSpec(shp, lambda m: (0,) * len(shp))
    return pl.pallas_call(
        _gmlp_kernel,
        out_shape=jax.ShapeDtypeStruct((S, GMLP_WIDTH), BF16),
        grid=(S // tm,),
        in_specs=[blk(COL_U), blk(COL_V), blk(COL_GZ),
                  full((1, GMLP_WIDTH)), full((1, GMLP_WIDTH)),
                  full((GMLP_GROUPS, GMLP_CHUNK, GMLP_CHUNK)), full((GMLP_CHUNK, GMLP_GROUPS))],
        out_specs=pl.BlockSpec((tm, GMLP_WIDTH), lambda m: (m, 0)),
        compiler_params=pltpu.CompilerParams(dimension_semantics=("arbitrary",),
                                             vmem_limit_bytes=VMEM_LIMIT),
        name="gmlp",
    )(p, p, p, ln_g, ln_b, ws, bs_t)


def _memkv_kernel(mem_ref, g_ref, w_ref, kv_ref):
    mf = mem_ref[...]
    ms = jnp.mean(mf * mf, axis=-1, keepdims=True)
    m = (mf * lax.rsqrt(ms + EPS) * g_ref[...]).astype(BF16)
    kv_ref[...] = _dot(m, w_ref[...]).astype(BF16)


def _memkv(mem2, g, w):
    M = mem2.shape[0]
    return pl.pallas_call(
        _memkv_kernel,
        out_shape=jax.ShapeDtypeStruct((M, 2 * XA_WIDTH), BF16),
        compiler_params=pltpu.CompilerParams(vmem_limit_bytes=VMEM_LIMIT),
        name="memkv",
    )(mem2, g, w)


def _xattn_kernel(q_ref, z_ref, kv_ref, o_ref):
    scale = XA_HEAD_DIM ** -0.5
    for h in range(XA_HEADS):
        cs = slice(h * LANES, (h + 1) * LANES)
        mk = kv_ref[:, h * LANES:(h + 1) * LANES]
        mv = kv_ref[:, XA_WIDTH + h * LANES:XA_WIDTH + (h + 1) * LANES]
        s = lax.dot_general(q_ref[:, cs], mk, (((1,), (1,)), ((), ())),
                            preferred_element_type=F32) * scale
        mx = jnp.max(s, axis=-1, keepdims=True)
        e = jnp.exp(s - mx)
        l = jnp.sum(e, axis=-1, keepdims=True)
        o = _dot(e.astype(BF16), mv) / l
        o_ref[:, cs] = (o * _silu(z_ref[:, cs].astype(F32))).astype(BF16)


def _xattn(p, kv):
    S = p.shape[0]
    M = kv.shape[0]
    tm = min(512, S)
    return pl.pallas_call(
        _xattn_kernel,
        out_shape=jax.ShapeDtypeStruct((S, XA_WIDTH), BF16),
        grid=(S // tm,),
        in_specs=[pl.BlockSpec((tm, XA_WIDTH), lambda m: (m, COL_CQ)),
                  pl.BlockSpec((tm, XA_WIDTH), lambda m: (m, COL_CZ)),
                  pl.BlockSpec((M, 2 * XA_WIDTH), lambda m: (0, 0))],
        out_specs=pl.BlockSpec((tm, XA_WIDTH), lambda m: (m, 0)),
        compiler_params=pltpu.CompilerParams(dimension_semantics=("arbitrary",),
                                             vmem_limit_bytes=VMEM_LIMIT),
        name="xattn",
    )(p, p, kv)


def _unit_lower_inverse(a, masks, eye):
    base_mask, merge_masks = masks
    x = jnp.where(base_mask, -a, 0.0)
    xb = x.astype(BF16)
    t = eye + x
    n = 2
    while n < INV_BASE:
        x = _dot(xb, xb)
        xb = x.astype(BF16)
        t = t + _dot(t.astype(BF16), xb)
        n *= 2
    for m in merge_masks:
        al = jnp.where(m, a, 0.0).astype(BF16)
        tb = t.astype(BF16)
        t = t - _dot(_dot(tb, al).astype(BF16), tb)
    return t


def _dn_kernel(q_ref, k_ref, v_ref, z_ref, ab_ref, cw_ref, alog_ref, dtb_ref, ng_ref,
               o_ref, state_ref, tail_ref):
    C = DN_BLOCK
    i = pl.program_id(0)

    @pl.when(i == 0)
    def _():
        state_ref[...] = jnp.zeros_like(state_ref)
        tail_ref[...] = jnp.zeros_like(tail_ref)

    row = lax.broadcasted_iota(jnp.int32, (C, C), 0)
    col = lax.broadcasted_iota(jnp.int32, (C, C), 1)
    tri = row >= col
    strict = row > col
    eye = (row == col).astype(F32)
    same = lambda b: (row // b) == (col // b)
    base_mask = same(INV_BASE)
    merge_masks = []
    b = INV_BASE
    while b < C:
        merge_masks.append(jnp.logical_and(same(2 * b), jnp.logical_not(same(b))))
        b *= 2
    masks = (base_mask, merge_masks)
    row8 = lax.broadcasted_iota(jnp.int32, (SUBLANES, LANES), 0)

    ab = ab_ref[...]
    lane = lax.broadcasted_iota(jnp.int32, (C, LANES), 1)
    xa = ab + dtb_ref[...]
    softplus = jnp.maximum(xa, 0.0) + jnp.log1p(jnp.exp(-jnp.abs(xa)))
    g = jnp.where(lane < DN_HEADS, -jnp.exp(alog_ref[...]) * softplus, 0.0)
    beta_t = jax.nn.sigmoid(ab)

    ones_l = tri.astype(BF16)
    g1 = g.astype(BF16)
    r1 = g - g1.astype(F32)
    g2 = r1.astype(BF16)
    g3 = (r1 - g2.astype(F32)).astype(BF16)
    gc = _dot(ones_l, g1) + _dot(ones_l, g2) + _dot(ones_l, g3)
    egc_t = jnp.exp(gc)
    gct8 = gc.T[0:SUBLANES, :]
    edl8 = jnp.exp(gct8[:, C - 1:C] - gct8)

    def conv_silu(ref, c0, h):
        cs = slice(h * LANES, (h + 1) * LANES)
        ts = slice(c0 + h * LANES, c0 + (h + 1) * LANES)
        cur = ref[:, cs].astype(F32)
        tail = tail_ref[:, ts]
        acc = cur * cw_ref[DN_CONV - 1:DN_CONV, ts]
        for s in range(1, DN_CONV):
            r = pltpu.roll(cur, s, 0)
            rt = pltpu.roll(tail, s, 0)
            first = jnp.where(row8 < s, rt, r[0:SUBLANES])
            sh = jnp.concatenate([first, r[SUBLANES:]], axis=0)
            acc = acc + sh * cw_ref[DN_CONV - 1 - s:DN_CONV - s, ts]
        tail_ref[:, ts] = cur[C - SUBLANES:C]
        return _silu(acc)

    for h in range(DN_HEADS):
        cs = slice(h * LANES, (h + 1) * LANES)
        q = conv_silu(q_ref, 0, h)
        k = conv_silu(k_ref, DN_WIDTH, h)
        v = conv_silu(v_ref, 2 * DN_WIDTH, h)
        q = q * (lax.rsqrt(jnp.sum(q * q, axis=-1, keepdims=True) + EPS) * (DN_HEAD_DIM ** -0.5))
        k = k * lax.rsqrt(jnp.sum(k * k, axis=-1, keepdims=True) + EPS)

        beta = beta_t[:, DN_HEADS + h:DN_HEADS + h + 1]
        gc_col = gc[:, h:h + 1]
        egc = egc_t[:, h:h + 1]
        egl = jnp.exp(gct8[h:h + 1, C - 1:C])
        gc_row = gct8[h:h + 1, :]
        decay = jnp.exp(jnp.where(tri, gc_col - gc_row, -jnp.inf))

        kb = k * beta
        vb = v * beta
        kt = k.T
        ktb = kt.astype(BF16)
        a = jnp.where(strict, _dot(kb.astype(BF16), ktb) * decay, 0.0)
        ai = _dot(q.astype(BF16), ktb) * decay
        t = _unit_lower_inverse(a, masks, eye)

        rhs = jnp.concatenate([vb, kb * egc], axis=1).astype(BF16)
        tv = _dot(t.astype(BF16), rhs)
        value = tv[:, :LANES]
        kcd = tv[:, LANES:]

        st = state_ref[h]
        stb = st.astype(BF16)
        v_new = value - _dot(kcd.astype(BF16), stb)
        vnb = v_new.astype(BF16)
        o = _dot((q * egc).astype(BF16), stb) + _dot(ai.astype(BF16), vnb)
        kdt = (kt * edl8[h:h + 1, :]).astype(BF16)
        state_ref[h] = st * egl + _dot(kdt, vnb)

        on = o * lax.rsqrt(jnp.mean(o * o, axis=-1, keepdims=True) + EPS) * ng_ref[...]
        o_ref[:, cs] = (on * _silu(z_ref[:, cs].astype(F32))).astype(BF16)


def _deltanet(p, ab, conv_w, alog, dtb, ng):
    S = p.shape[0]
    C = DN_BLOCK
    blk = lambda j: pl.BlockSpec((C, DN_WIDTH), lambda m, j=j: (m, j))
    full = lambda shp: pl.BlockSpec(shp, lambda m: (0,) * len(shp))
    return pl.pallas_call(
        _dn_kernel,
        out_shape=jax.ShapeDtypeStruct((S, DN_WIDTH), BF16),
        grid=(S // C,),
        in_specs=[blk(COL_Q // 2), blk(COL_K // 2), blk(COL_DV // 2), blk(COL_DZ // 2),
                  pl.BlockSpec((C, LANES), lambda m: (m, 0)),
                  full((DN_CONV, 3 * DN_WIDTH)), full((1, LANES)), full((1, LANES)),
                  full((1, DN_HEAD_DIM))],
        out_specs=pl.BlockSpec((C, DN_WIDTH), lambda m: (m, 0)),
        scratch_shapes=[pltpu.VMEM((DN_HEADS, DN_HEAD_DIM, DN_HEAD_DIM), F32),
                        pltpu.VMEM((SUBLANES, 3 * DN_WIDTH), F32)],
        compiler_params=pltpu.CompilerParams(dimension_semantics=("arbitrary",),
                                             vmem_limit_bytes=VMEM_LIMIT),
        name="deltanet",
    )(p, p, p, p, ab, conv_w, alog, dtb, ng)


def _out_kernel(x_ref, a_ref, b_ref, c_ref, w_ref, fg_ref, y_ref):
    acc = _dot(a_ref[...], w_ref[0:GMLP_WIDTH, :])
    acc = acc + _dot(b_ref[...], w_ref[GMLP_WIDTH:GMLP_WIDTH + DN_WIDTH, :])
    acc = acc + _dot(c_ref[...], w_ref[GMLP_WIDTH + DN_WIDTH:, :])
    r = x_ref[...] + acc
    ms = jnp.mean(r * r, axis=-1, keepdims=True)
    y_ref[...] = r * lax.rsqrt(ms + EPS) * fg_ref[...]


def _outproj(x2, oa, ob, oc, w_out, fg):
    S = x2.shape[0]
    tm = min(512, S)
    row = lambda w: pl.BlockSpec((tm, w), lambda m: (m, 0))
    return pl.pallas_call(
        _out_kernel,
        out_shape=jax.ShapeDtypeStruct((S, D_MODEL), F32),
        grid=(S // tm,),
        in_specs=[row(D_MODEL), row(GMLP_WIDTH), row(DN_WIDTH), row(XA_WIDTH),
                  pl.BlockSpec((D_MODEL, D_MODEL), lambda m: (0, 0)),
                  pl.BlockSpec((1, D_MODEL), lambda m: (0, 0))],
        out_specs=row(D_MODEL),
        compiler_params=pltpu.CompilerParams(dimension_semantics=("arbitrary",),
                                             vmem_limit_bytes=VMEM_LIMIT),
        name="outproj",
    )(x2, oa, ob, oc, w_out, fg)


def _pad_lanes(v):
    return jnp.pad(v.astype(F32), (0, LANES - v.shape[0]))[None, :]


def kernel(x, mem, ln_g, w_in, gmlp_ln_g, gmlp_ln_b, gmlp_ws, gmlp_bs, conv_w, dn_a_log,
           dn_dt_bias, dn_norm_g, mem_norm_g, w_mem_kv, w_out, final_g):
    B, S, _ = x.shape
    assert B == 1 and ln_g.shape[0] == 1 and S % DN_BLOCK == 0
    x2 = x[0]
    w = w_in[0]
    ab0 = MAIN_WIDTH - 2 * XA_WIDTH
    dn0 = 3 * GMLP_WIDTH
    w_main = jnp.concatenate([w[:, dn0:ab0], w[:, :dn0], w[:, ab0 + 2 * DN_HEADS:]],
                             axis=1).astype(BF16)
    w_ab = jnp.pad(w[:, ab0:ab0 + 2 * DN_HEADS], ((0, 0), (0, LANES - 2 * DN_HEADS))).astype(BF16)

    p, ab = _proj(x2, ln_g, w_main, w_ab)
    out_a = _gmlp(p, gmlp_ln_g, gmlp_ln_b, gmlp_ws[0], gmlp_bs[0].T)
    kv = _memkv(mem[0], mem_norm_g, w_mem_kv[0].astype(BF16))
    out_c = _xattn(p, kv)
    out_b = _deltanet(p, ab, conv_w[0], _pad_lanes(dn_a_log[0]), _pad_lanes(dn_dt_bias[0]),
                      dn_norm_g)
    y = _outproj(x2, out_a, out_b, out_c, w_out[0].astype(BF16), final_g[None, :])
    return y[None]
```

```python
import functools

import jax
import jax.numpy as jnp
from jax import lax
from jax.experimental import pallas as pl
from jax.experimental.pallas import tpu as pltpu

F32 = jnp.float32
BF16 = jnp.bfloat16

D_MODEL = 2048
GMLP_CHUNK = 128
GMLP_GROUPS = 4
GMLP_WIDTH = 512
DN_HEADS = 8
DN_HEAD_DIM = 128
DN_WIDTH = 1024
DN_CONV = 4
XA_HEADS = 4
XA_HEAD_DIM = 128
XA_WIDTH = 512
EPS = 1e-6

LANES = 128
SUBLANES = 8
DN_BLOCK = 128
INV_BASE = 16

MAIN_WIDTH = 3 * GMLP_WIDTH + 4 * DN_WIDTH + 2 * XA_WIDTH
PROJ_TN = 512
COL_Q, COL_K, COL_DV, COL_DZ = 0, 2, 4, 6
COL_U, COL_V, COL_GZ = 8, 9, 10
COL_CQ, COL_CZ = 11, 12

VMEM_LIMIT = 48 * 1024 * 1024


def _dot(a, b):
    return jnp.dot(a, b, preferred_element_type=F32)


def _silu(x):
    return x * jax.nn.sigmoid(x)


def _proj_kernel(x_ref, g_ref, w_ref, wab_ref, p_ref, ab_ref, h_ref, *, row_chunk):
    n = pl.program_id(1)
    tm = x_ref.shape[0]

    @pl.when(n == 0)
    def _():
        def body(c, carry):
            r0 = pl.multiple_of(c * row_chunk, row_chunk)
            xf = x_ref[pl.ds(r0, row_chunk), :]
            ms = jnp.mean(xf * xf, axis=-1, keepdims=True)
            h_ref[pl.ds(r0, row_chunk), :] = (xf * lax.rsqrt(ms + EPS) * g_ref[...]).astype(BF16)
            return carry

        lax.fori_loop(0, tm // row_chunk, body, 0)
        ab_ref[...] = _dot(h_ref[...], wab_ref[...])

    p_ref[...] = _dot(h_ref[...], w_ref[...]).astype(BF16)


def _proj(x2, ln_g, w_main, w_ab):
    S = x2.shape[0]
    tm = min(1024, S)
    grid = (S // tm, MAIN_WIDTH // PROJ_TN)
    return pl.pallas_call(
        functools.partial(_proj_kernel, row_chunk=128),
        out_shape=(jax.ShapeDtypeStruct((S, MAIN_WIDTH), BF16),
                   jax.ShapeDtypeStruct((S, LANES), F32)),
        grid=grid,
        in_specs=[
            pl.BlockSpec((tm, D_MODEL), lambda m, n: (m, 0)),
            pl.BlockSpec((1, D_MODEL), lambda m, n: (0, 0)),
            pl.BlockSpec((D_MODEL, PROJ_TN), lambda m, n: (0, n)),
            pl.BlockSpec((D_MODEL, LANES), lambda m, n: (0, 0)),
        ],
        out_specs=(
            pl.BlockSpec((tm, PROJ_TN), lambda m, n: (m, n)),
            pl.BlockSpec((tm, LANES), lambda m, n: (m, 0)),
        ),
        scratch_shapes=[pltpu.VMEM((tm, D_MODEL), BF16)],
        compiler_params=pltpu.CompilerParams(
            dimension_semantics=("arbitrary", "arbitrary"),
            vmem_limit_bytes=VMEM_LIMIT),
        name="proj",
    )(x2, ln_g, w_main, w_ab)


def _gmlp_kernel(u_ref, v_ref, z_ref, lng_ref, lnb_ref, ws_ref, bst_ref, o_ref):
    tm = u_ref.shape[0]
    T = GMLP_CHUNK
    v = jax.nn.gelu(v_ref[...].astype(F32))
    mu = jnp.mean(v, axis=-1, keepdims=True)
    vc = v - mu
    var = jnp.mean(vc * vc, axis=-1, keepdims=True)
    vn = (vc * lax.rsqrt(var + EPS) * lng_ref[...] + lnb_ref[...]).astype(BF16)
    row = lax.broadcasted_iota(jnp.int32, (T, T), 0)
    col = lax.broadcasted_iota(jnp.int32, (T, T), 1)
    causal = row >= col
    for g in range(GMLP_GROUPS):
        w = jnp.where(causal, ws_ref[g], 0.0).astype(BF16)
        bias = bst_ref[:, g:g + 1]
        cs = slice(g * LANES, (g + 1) * LANES)
        for c in range(tm // T):
            rs = slice(c * T, (c + 1) * T)
            s = _dot(w, vn[rs, cs]) + bias
            u = jax.nn.gelu(u_ref[rs, cs].astype(F32))
            o_ref[rs, cs] = (u * s * _silu(z_ref[rs, cs].astype(F32))).astype(BF16)


def _gmlp(p, ln_g, ln_b, ws, bs_t):
    S = p.shape[0]
    tm = min(512, S)
    blk = lambda j: pl.BlockSpec((tm, GMLP_WIDTH), lambda m, j=j: (m, j))
    full = lambda shp: pl.BlockSpec(shp, lambda m: (0,) * len(shp))
    return pl.pallas_call(
        _gmlp_kernel,
        out_shape=jax.ShapeDtypeStruct((S, GMLP_WIDTH), BF16),
        grid=(S // tm,),
        in_specs=[blk(COL_U), blk(COL_V), blk(COL_GZ),
                  full((1, GMLP_WIDTH)), full((1, GMLP_WIDTH)),
                  full((GMLP_GROUPS, GMLP_CHUNK, GMLP_CHUNK)), full((GMLP_CHUNK, GMLP_GROUPS))],
        out_specs=pl.BlockSpec((tm, GMLP_WIDTH), lambda m: (m, 0)),
        compiler_params=pltpu.CompilerParams(dimension_semantics=("arbitrary",),
                                             vmem_limit_bytes=VMEM_LIMIT),
        name="gmlp",
    )(p, p, p, ln_g, ln_b, ws, bs_t)


def _memkv_kernel(mem_ref, g_ref, w_ref, kv_ref):
    mf = mem_ref[...]
    ms = jnp.mean(mf * mf, axis=-1, keepdims=True)
    m = (mf * lax.rsqrt(ms + EPS) * g_ref[...]).astype(BF16)
    kv_ref[...] = _dot(m, w_ref[...]).astype(BF16)


def _memkv(mem2, g, w):
    M = mem2.shape[0]
    return pl.pallas_call(
        _memkv_kernel,
        out_shape=jax.ShapeDtypeStruct((M, 2 * XA_WIDTH), BF16),
        compiler_params=pltpu.CompilerParams(vmem_limit_bytes=VMEM_LIMIT),
        name="memkv",
    )(mem2, g, w)


def _xattn_kernel(q_ref, z_ref, kv_ref, o_ref):
    scale = XA_HEAD_DIM ** -0.5
    for h in range(XA_HEADS):
        cs = slice(h * LANES, (h + 1) * LANES)
        mk = kv_ref[:, h * LANES:(h + 1) * LANES]
        mv = kv_ref[:, XA_WIDTH + h * LANES:XA_WIDTH + (h + 1) * LANES]
        s = lax.dot_general(q_ref[:, cs], mk, (((1,), (1,)), ((), ())),
                            preferred_element_type=F32) * scale
        mx = jnp.max(s, axis=-1, keepdims=True)
        e = jnp.exp(s - mx)
        l = jnp.sum(e, axis=-1, keepdims=True)
        o = _dot(e.astype(BF16), mv) / l
        o_ref[:, cs] = (o * _silu(z_ref[:, cs].astype(F32))).astype(BF16)


def _xattn(p, kv):
    S = p.shape[0]
    M = kv.shape[0]
    tm = min(512, S)
    return pl.pallas_call(
        _xattn_kernel,
        out_shape=jax.ShapeDtypeStruct((S, XA_WIDTH), BF16),
        grid=(S // tm,),
        in_specs=[pl.BlockSpec((tm, XA_WIDTH), lambda m: (m, COL_CQ)),
                  pl.BlockSpec((tm, XA_WIDTH), lambda m: (m, COL_CZ)),
                  pl.BlockSpec((M, 2 * XA_WIDTH), lambda m: (0, 0))],
        out_specs=pl.BlockSpec((tm, XA_WIDTH), lambda m: (m, 0)),
        compiler_params=pltpu.CompilerParams(dimension_semantics=("arbitrary",),
                                             vmem_limit_bytes=VMEM_LIMIT),
        name="xattn",
    )(p, p, kv)


def _unit_lower_inverse(a_list, masks, eye):
    base_mask, merge_masks = masks
    n_mat = range(len(a_list))
    x = [jnp.where(base_mask, -a, 0.0) for a in a_list]
    xb = [v.astype(BF16) for v in x]
    t = [eye + v for v in x]
    n = 2
    while n < INV_BASE:
        x = [_dot(xb[j], xb[j]) for j in n_mat]
        xb = [v.astype(BF16) for v in x]
        tx = [_dot(t[j].astype(BF16), xb[j]) for j in n_mat]
        t = [t[j] + tx[j] for j in n_mat]
        n *= 2
    for m in merge_masks:
        al = [jnp.where(m, a, 0.0).astype(BF16) for a in a_list]
        tb = [v.astype(BF16) for v in t]
        u = [_dot(tb[j], al[j]).astype(BF16) for j in n_mat]
        w = [_dot(u[j], tb[j]) for j in n_mat]
        t = [t[j] - w[j] for j in n_mat]
    return t


def _dn_kernel(q_ref, k_ref, v_ref, z_ref, ab_ref, cw_ref, alog_ref, dtb_ref, ng_ref,
               o_ref, state_ref, tail_ref):
    C = DN_BLOCK
    i = pl.program_id(0)

    @pl.when(i == 0)
    def _():
        state_ref[...] = jnp.zeros_like(state_ref)
        tail_ref[...] = jnp.zeros_like(tail_ref)

    row = lax.broadcasted_iota(jnp.int32, (C, C), 0)
    col = lax.broadcasted_iota(jnp.int32, (C, C), 1)
    tri = row >= col
    strict = row > col
    eye = (row == col).astype(F32)
    same = lambda b: (row // b) == (col // b)
    base_mask = same(INV_BASE)
    merge_masks = []
    b = INV_BASE
    while b < C:
        merge_masks.append(jnp.logical_and(same(2 * b), jnp.logical_not(same(b))))
        b *= 2
    masks = (base_mask, merge_masks)
    row8 = lax.broadcasted_iota(jnp.int32, (SUBLANES, LANES), 0)

    ab = ab_ref[...]
    lane = lax.broadcasted_iota(jnp.int32, (C, LANES), 1)
    xa = ab + dtb_ref[...]
    softplus = jnp.maximum(xa, 0.0) + jnp.log1p(jnp.exp(-jnp.abs(xa)))
    g = jnp.where(lane < DN_HEADS, -jnp.exp(alog_ref[...]) * softplus, 0.0)
    beta_t = jax.nn.sigmoid(ab)

    ones_l = tri.astype(BF16)
    g1 = g.astype(BF16)
    r1 = g - g1.astype(F32)
    g2 = r1.astype(BF16)
    g3 = (r1 - g2.astype(F32)).astype(BF16)
    gc = _dot(ones_l, g1) + _dot(ones_l, g2) + _dot(ones_l, g3)
    egc_t = jnp.exp(gc)
    gct8 = gc.T[0:SUBLANES, :]
    edl8 = jnp.exp(gct8[:, C - 1:C] - gct8)

    def conv_silu(ref, c0, h):
        cs = slice(h * LANES, (h + 1) * LANES)
        ts = slice(c0 + h * LANES, c0 + (h + 1) * LANES)
        cur = ref[:, cs].astype(F32)
        tail = tail_ref[:, ts]
        acc = cur * cw_ref[DN_CONV - 1:DN_CONV, ts]
        for s in range(1, DN_CONV):
            r = pltpu.roll(cur, s, 0)
            rt = pltpu.roll(tail, s, 0)
            first = jnp.where(row8 < s, rt, r[0:SUBLANES])
            sh = jnp.concatenate([first, r[SUBLANES:]], axis=0)
            acc = acc + sh * cw_ref[DN_CONV - 1 - s:DN_CONV - s, ts]
        tail_ref[:, ts] = cur[C - SUBLANES:C]
        return _silu(acc)

    heads = range(DN_HEADS)
    qb, qgb, kbb, ktb, kdtb, rhs, decay = [], [], [], [], [], [], []
    for h in heads:
        q = conv_silu(q_ref, 0, h)
        k = conv_silu(k_ref, DN_WIDTH, h)
        v = conv_silu(v_ref, 2 * DN_WIDTH, h)
        q = q * (lax.rsqrt(jnp.sum(q * q, axis=-1, keepdims=True) + EPS) * (DN_HEAD_DIM ** -0.5))
        k = k * lax.rsqrt(jnp.sum(k * k, axis=-1, keepdims=True) + EPS)
        beta = beta_t[:, DN_HEADS + h:DN_HEADS + h + 1]
        egc = egc_t[:, h:h + 1]
        kb = k * beta
        kt = k.T
        qb.append(q.astype(BF16))
        qgb.append((q * egc).astype(BF16))
        kbb.append(kb.astype(BF16))
        ktb.append(kt.astype(BF16))
        kdtb.append((kt * edl8[h:h + 1, :]).astype(BF16))
        rhs.append(jnp.concatenate([v * beta, kb * egc], axis=1).astype(BF16))
        decay.append(jnp.exp(jnp.where(tri, gc[:, h:h + 1] - gct8[h:h + 1, :], -jnp.inf)))

    kk = [_dot(kbb[h], ktb[h]) for h in heads]
    qk = [_dot(qb[h], ktb[h]) for h in heads]
    a = [jnp.where(strict, kk[h] * decay[h], 0.0) for h in heads]
    aib = [(qk[h] * decay[h]).astype(BF16) for h in heads]
    t = _unit_lower_inverse(a, masks, eye)

    tv = [_dot(t[h].astype(BF16), rhs[h]) for h in heads]
    stb = [state_ref[h].astype(BF16) for h in heads]
    ks = [_dot(tv[h][:, LANES:].astype(BF16), stb[h]) for h in heads]
    o_inter = [_dot(qgb[h], stb[h]) for h in heads]
    vnb = [(tv[h][:, :LANES] - ks[h]).astype(BF16) for h in heads]
    o_intra = [_dot(aib[h], vnb[h]) for h in heads]
    upd = [_dot(kdtb[h], vnb[h]) for h in heads]
    for h in heads:
        cs = slice(h * LANES, (h + 1) * LANES)
        egl = jnp.exp(gct8[h:h + 1, C - 1:C])
        state_ref[h] = state_ref[h] * egl + upd[h]
        o = o_inter[h] + o_intra[h]
        on = o * lax.rsqrt(jnp.mean(o * o, axis=-1, keepdims=True) + EPS) * ng_ref[...]
        o_ref[:, cs] = (on * _silu(z_ref[:, cs].astype(F32))).astype(BF16)


def _deltanet(p, ab, conv_w, alog, dtb, ng):
    S = p.shape[0]
    C = DN_BLOCK
    blk = lambda j: pl.BlockSpec((C, DN_WIDTH), lambda m, j=j: (m, j))
    full = lambda shp: pl.BlockSpec(shp, lambda m: (0,) * len(shp))
    return pl.pallas_call(
        _dn_kernel,
        out_shape=jax.ShapeDtypeStruct((S, DN_WIDTH), BF16),
        grid=(S // C,),
        in_specs=[blk(COL_Q // 2), blk(COL_K // 2), blk(COL_DV // 2), blk(COL_DZ // 2),
                  pl.BlockSpec((C, LANES), lambda m: (m, 0)),
                  full((DN_CONV, 3 * DN_WIDTH)), full((1, LANES)), full((1, LANES)),
                  full((1, DN_HEAD_DIM))],
        out_specs=pl.BlockSpec((C, DN_WIDTH), lambda m: (m, 0)),
        scratch_shapes=[pltpu.VMEM((DN_HEADS, DN_HEAD_DIM, DN_HEAD_DIM), F32),
                        pltpu.VMEM((SUBLANES, 3 * DN_WIDTH), F32)],
        compiler_params=pltpu.CompilerParams(dimension_semantics=("arbitrary",),
                                             vmem_limit_bytes=VMEM_LIMIT),
        name="deltanet",
    )(p, p, p, p, ab, conv_w, alog, dtb, ng)


def _out_kernel(x_ref, a_ref, b_ref, c_ref, w_ref, fg_ref, y_ref):
    acc = _dot(a_ref[...], w_ref[0:GMLP_WIDTH, :])
    acc = acc + _dot(b_ref[...], w_ref[GMLP_WIDTH:GMLP_WIDTH + DN_WIDTH, :])
    acc = acc + _dot(c_ref[...], w_ref[GMLP_WIDTH + DN_WIDTH:, :])
    r = x_ref[...] + acc
    ms = jnp.mean(r * r, axis=-1, keepdims=True)
    y_ref[...] = r * lax.rsqrt(ms + EPS) * fg_ref[...]


def _outproj(x2, oa, ob, oc, w_out, fg):
    S = x2.shape[0]
    tm = min(512, S)
    row = lambda w: pl.BlockSpec((tm, w), lambda m: (m, 0))
    return pl.pallas_call(
        _out_kernel,
        out_shape=jax.ShapeDtypeStruct((S, D_MODEL), F32),
        grid=(S // tm,),
        in_specs=[row(D_MODEL), row(GMLP_WIDTH), row(DN_WIDTH), row(XA_WIDTH),
                  pl.BlockSpec((D_MODEL, D_MODEL), lambda m: (0, 0)),
                  pl.BlockSpec((1, D_MODEL), lambda m: (0, 0))],
        out_specs=row(D_MODEL),
        compiler_params=pltpu.CompilerParams(dimension_semantics=("arbitrary",),
                                             vmem_limit_bytes=VMEM_LIMIT),
        name="outproj",
    )(x2, oa, ob, oc, w_out, fg)


def _pad_lanes(v):
    return jnp.pad(v.astype(F32), (0, LANES - v.shape[0]))[None, :]


def kernel(x, mem, ln_g, w_in, gmlp_ln_g, gmlp_ln_b, gmlp_ws, gmlp_bs, conv_w, dn_a_log,
           dn_dt_bias, dn_norm_g, mem_norm_g, w_mem_kv, w_out, final_g):
    B, S, _ = x.shape
    assert B == 1 and ln_g.shape[0] == 1 and S % DN_BLOCK == 0
    x2 = x[0]
    w = w_in[0]
    ab0 = MAIN_WIDTH - 2 * XA_WIDTH
    dn0 = 3 * GMLP_WIDTH
    w_main = jnp.concatenate([w[:, dn0:ab0], w[:, :dn0], w[:, ab0 + 2 * DN_HEADS:]],
                             axis=1).astype(BF16)
    w_ab = jnp.pad(w[:, ab0:ab0 + 2 * DN_HEADS], ((0, 0), (0, LANES - 2 * DN_HEADS))).astype(BF16)

    p, ab = _proj(x2, ln_g, w_main, w_ab)
    out_a = _gmlp(p, gmlp_ln_g, gmlp_ln_b, gmlp_ws[0], gmlp_bs[0].T)
    kv = _memkv(mem[0], mem_norm_g, w_mem_kv[0].astype(BF16))
    out_c = _xattn(p, kv)
    out_b = _deltanet(p, ab, conv_w[0], _pad_lanes(dn_a_log[0]), _pad_lanes(dn_dt_bias[0]),
                      dn_norm_g)
    y = _outproj(x2, out_a, out_b, out_c, w_out[0].astype(BF16), final_g[None, :])
    return y[None]
```

```python
import functools

import jax
import jax.numpy as jnp
from jax import lax
from jax.experimental import pallas as pl
from jax.experimental.pallas import tpu as pltpu

F32 = jnp.float32
BF16 = jnp.bfloat16

D_MODEL = 2048
GMLP_CHUNK = 128
GMLP_GROUPS = 4
GMLP_WIDTH = 512
DN_HEADS = 8
DN_HEAD_DIM = 128
DN_WIDTH = 1024
DN_CONV = 4
XA_HEADS = 4
XA_HEAD_DIM = 128
XA_WIDTH = 512
EPS = 1e-6

LANES = 128
SUBLANES = 8
DN_BLOCK = 128
INV_BASE = 16

MAIN_WIDTH = 3 * GMLP_WIDTH + 4 * DN_WIDTH + 2 * XA_WIDTH
PROJ_TN = 512
PROJ_DIRECT_TILES = (3 * GMLP_WIDTH + 4 * DN_WIDTH) // PROJ_TN
COL_Q, COL_K, COL_DV, COL_DZ = 0, 2, 4, 6
COL_U, COL_V, COL_GZ = 8, 9, 10
COL_CQ, COL_CZ = 11, 12

VMEM_LIMIT = 48 * 1024 * 1024


def _dot(a, b):
    return jnp.dot(a, b, preferred_element_type=F32)


def _silu(x):
    return x * jax.nn.sigmoid(x)


def _proj_kernel(x_ref, g_ref, wf_ref, wc_ref, wab_ref, p_ref, ab_ref, h_ref, *, row_chunk):
    n = pl.program_id(1)
    tm = x_ref.shape[0]

    @pl.when(n == 0)
    def _():
        def body(c, carry):
            r0 = pl.multiple_of(c * row_chunk, row_chunk)
            xf = x_ref[pl.ds(r0, row_chunk), :]
            ms = jnp.mean(xf * xf, axis=-1, keepdims=True)
            h_ref[pl.ds(r0, row_chunk), :] = (xf * lax.rsqrt(ms + EPS) * g_ref[...]).astype(BF16)
            return carry

        lax.fori_loop(0, tm // row_chunk, body, 0)
        ab_ref[...] = _dot(h_ref[...], wab_ref[...])

    @pl.when(n < PROJ_DIRECT_TILES)
    def _():
        p_ref[...] = _dot(h_ref[...], wf_ref[...].astype(BF16)).astype(BF16)

    @pl.when(n >= PROJ_DIRECT_TILES)
    def _():
        p_ref[...] = _dot(h_ref[...], wc_ref[...]).astype(BF16)


def _proj_direct_block(n):
    dn_tiles = 4 * DN_WIDTH // PROJ_TN
    gm_tiles = 3 * GMLP_WIDTH // PROJ_TN
    last = PROJ_DIRECT_TILES - 1 - dn_tiles
    return jnp.where(n < dn_tiles, n + gm_tiles, jnp.minimum(n - dn_tiles, last))


def _proj(x2, ln_g, w_in2, w_c, w_ab):
    S = x2.shape[0]
    tm = min(1024, S)
    grid = (S // tm, MAIN_WIDTH // PROJ_TN)
    return pl.pallas_call(
        functools.partial(_proj_kernel, row_chunk=128),
        out_shape=(jax.ShapeDtypeStruct((S, MAIN_WIDTH), BF16),
                   jax.ShapeDtypeStruct((S, LANES), F32)),
        grid=grid,
        in_specs=[
            pl.BlockSpec((tm, D_MODEL), lambda m, n: (m, 0)),
            pl.BlockSpec((1, D_MODEL), lambda m, n: (0, 0)),
            pl.BlockSpec((D_MODEL, PROJ_TN), lambda m, n: (0, _proj_direct_block(n))),
            pl.BlockSpec((D_MODEL, PROJ_TN), lambda m, n: (0, jnp.maximum(n - PROJ_DIRECT_TILES, 0))),
            pl.BlockSpec((D_MODEL, LANES), lambda m, n: (0, 0)),
        ],
        out_specs=(
            pl.BlockSpec((tm, PROJ_TN), lambda m, n: (m, n)),
            pl.BlockSpec((tm, LANES), lambda m, n: (m, 0)),
        ),
        scratch_shapes=[pltpu.VMEM((tm, D_MODEL), BF16)],
        compiler_params=pltpu.CompilerParams(
            dimension_semantics=("arbitrary", "arbitrary"),
            vmem_limit_bytes=VMEM_LIMIT),
        name="proj",
    )(x2, ln_g, w_in2, w_c, w_ab)


def _gmlp_kernel(u_ref, v_ref, z_ref, lng_ref, lnb_ref, ws_ref, bst_ref, o_ref):
    tm = u_ref.shape[0]
    T = GMLP_CHUNK
    v = jax.nn.gelu(v_ref[...].astype(F32))
    mu = jnp.mean(v, axis=-1, keepdims=True)
    vc = v - mu
    var = jnp.mean(vc * vc, axis=-1, keepdims=True)
    vn = (vc * lax.rsqrt(var + EPS) * lng_ref[...] + lnb_ref[...]).astype(BF16)
    row = lax.broadcasted_iota(jnp.int32, (T, T), 0)
    col = lax.broadcasted_iota(jnp.int32, (T, T), 1)
    causal = row >= col
    for g in range(GMLP_GROUPS):
        w = jnp.where(causal, ws_ref[g], 0.0).astype(BF16)
        bias = bst_ref[:, g:g + 1]
        cs = slice(g * LANES, (g + 1) * LANES)
        for c in range(tm // T):
            rs = slice(c * T, (c + 1) * T)
            s = _dot(w, vn[rs, cs]) + bias
            u = jax.nn.gelu(u_ref[rs, cs].astype(F32))
            o_ref[rs, cs] = (u * s * _silu(z_ref[rs, cs].astype(F32))).astype(BF16)


def _gmlp(p, ln_g, ln_b, ws, bs_t):
    S = p.shape[0]
    tm = min(512, S)
    blk = lambda j: pl.BlockSpec((tm, GMLP_WIDTH), lambda m, j=j: (m, j))
    full = lambda shp: pl.BlockSpec(shp, lambda m: (0,) * len(shp))
    return pl.pallas_call(
        _gmlp_kernel,
        out_shape=jax.ShapeDtypeStruct((S, GMLP_WIDTH), BF16),
        grid=(S // tm,),
        in_specs=[blk(COL_U), blk(COL_V), blk(COL_GZ),
                  full((1, GMLP_WIDTH)), full((1, GMLP_WIDTH)),
                  full((GMLP_GROUPS, GMLP_CHUNK, GMLP_CHUNK)), full((GMLP_CHUNK, GMLP_GROUPS))],
        out_specs=pl.BlockSpec((tm, GMLP_WIDTH), lambda m: (m, 0)),
        compiler_params=pltpu.CompilerParams(dimension_semantics=("arbitrary",),
                                             vmem_limit_bytes=VMEM_LIMIT),
        name="gmlp",
    )(p, p, p, ln_g, ln_b, ws, bs_t)


def _memkv_kernel(mem_ref, g_ref, w_ref, kv_ref):
    mf = mem_ref[...]
    ms = jnp.mean(mf * mf, axis=-1, keepdims=True)
    m = (mf * lax.rsqrt(ms + EPS) * g_ref[...]).astype(BF16)
    kv_ref[...] = _dot(m, w_ref[...]).astype(BF16)


def _memkv(mem2, g, w):
    M = mem2.shape[0]
    return pl.pallas_call(
        _memkv_kernel,
        out_shape=jax.ShapeDtypeStruct((M, 2 * XA_WIDTH), BF16),
        compiler_params=pltpu.CompilerParams(vmem_limit_bytes=VMEM_LIMIT),
        name="memkv",
    )(mem2, g, w)


def _xattn_kernel(q_ref, z_ref, kv_ref, o_ref):
    scale = XA_HEAD_DIM ** -0.5
    for h in range(XA_HEADS):
        cs = slice(h * LANES, (h + 1) * LANES)
        mk = kv_ref[:, h * LANES:(h + 1) * LANES]
        mv = kv_ref[:, XA_WIDTH + h * LANES:XA_WIDTH + (h + 1) * LANES]
        s = lax.dot_general(q_ref[:, cs], mk, (((1,), (1,)), ((), ())),
                            preferred_element_type=F32) * scale
        mx = jnp.max(s, axis=-1, keepdims=True)
        e = jnp.exp(s - mx)
        l = jnp.sum(e, axis=-1, keepdims=True)
        o = _dot(e.astype(BF16), mv) / l
        o_ref[:, cs] = (o * _silu(z_ref[:, cs].astype(F32))).astype(BF16)


def _xattn(p, kv):
    S = p.shape[0]
    M = kv.shape[0]
    tm = min(512, S)
    return pl.pallas_call(
        _xattn_kernel,
        out_shape=jax.ShapeDtypeStruct((S, XA_WIDTH), BF16),
        grid=(S // tm,),
        in_specs=[pl.BlockSpec((tm, XA_WIDTH), lambda m: (m, COL_CQ)),
                  pl.BlockSpec((tm, XA_WIDTH), lambda m: (m, COL_CZ)),
                  pl.BlockSpec((M, 2 * XA_WIDTH), lambda m: (0, 0))],
        out_specs=pl.BlockSpec((tm, XA_WIDTH), lambda m: (m, 0)),
        compiler_params=pltpu.CompilerParams(dimension_semantics=("arbitrary",),
                                             vmem_limit_bytes=VMEM_LIMIT),
        name="xattn",
    )(p, p, kv)


def _unit_lower_inverse(a_list, masks, eye):
    base_mask, merge_masks = masks
    n_mat = range(len(a_list))
    x = [jnp.where(base_mask, -a, 0.0) for a in a_list]
    xb = [v.astype(BF16) for v in x]
    t = [eye + v for v in x]
    n = 2
    while n < INV_BASE:
        x = [_dot(xb[j], xb[j]) for j in n_mat]
        xb = [v.astype(BF16) for v in x]
        tx = [_dot(t[j].astype(BF16), xb[j]) for j in n_mat]
        t = [t[j] + tx[j] for j in n_mat]
        n *= 2
    for m in merge_masks:
        al = [jnp.where(m, a, 0.0).astype(BF16) for a in a_list]
        tb = [v.astype(BF16) for v in t]
        u = [_dot(tb[j], al[j]).astype(BF16) for j in n_mat]
        w = [_dot(u[j], tb[j]) for j in n_mat]
        t = [t[j] - w[j] for j in n_mat]
    return t


def _dn_kernel(q_ref, k_ref, v_ref, z_ref, ab_ref, cw_ref, alog_ref, dtb_ref, ng_ref,
               o_ref, state_ref, tail_ref):
    C = DN_BLOCK
    i = pl.program_id(0)

    @pl.when(i == 0)
    def _():
        state_ref[...] = jnp.zeros_like(state_ref)
        tail_ref[...] = jnp.zeros_like(tail_ref)

    row = lax.broadcasted_iota(jnp.int32, (C, C), 0)
    col = lax.broadcasted_iota(jnp.int32, (C, C), 1)
    tri = row >= col
    strict = row > col
    eye = (row == col).astype(F32)
    same = lambda b: (row // b) == (col // b)
    base_mask = same(INV_BASE)
    merge_masks = []
    b = INV_BASE
    while b < C:
        merge_masks.append(jnp.logical_and(same(2 * b), jnp.logical_not(same(b))))
        b *= 2
    masks = (base_mask, merge_masks)
    row8 = lax.broadcasted_iota(jnp.int32, (SUBLANES, LANES), 0)

    ab = ab_ref[...]
    lane = lax.broadcasted_iota(jnp.int32, (C, LANES), 1)
    xa = ab + dtb_ref[...]
    softplus = jnp.maximum(xa, 0.0) + jnp.log1p(jnp.exp(-jnp.abs(xa)))
    g = jnp.where(lane < DN_HEADS, -jnp.exp(alog_ref[...]) * softplus, 0.0)
    beta_t = jax.nn.sigmoid(ab)

    ones_l = tri.astype(BF16)
    g1 = g.astype(BF16)
    r1 = g - g1.astype(F32)
    g2 = r1.astype(BF16)
    g3 = (r1 - g2.astype(F32)).astype(BF16)
    gc = _dot(ones_l, g1) + _dot(ones_l, g2) + _dot(ones_l, g3)
    egc_t = jnp.exp(gc)
    gct8 = gc.T[0:SUBLANES, :]
    edl8 = jnp.exp(gct8[:, C - 1:C] - gct8)

    def conv_silu(ref, c0, h):
        cs = slice(h * LANES, (h + 1) * LANES)
        ts = slice(c0 + h * LANES, c0 + (h + 1) * LANES)
        cur = ref[:, cs].astype(F32)
        tail = tail_ref[:, ts]
        acc = cur * cw_ref[DN_CONV - 1:DN_CONV, ts]
        for s in range(1, DN_CONV):
            r = pltpu.roll(cur, s, 0)
            rt = pltpu.roll(tail, s, 0)
            first = jnp.where(row8 < s, rt, r[0:SUBLANES])
            sh = jnp.concatenate([first, r[SUBLANES:]], axis=0)
            acc = acc + sh * cw_ref[DN_CONV - 1 - s:DN_CONV - s, ts]
        tail_ref[:, ts] = cur[C - SUBLANES:C]
        return _silu(acc)

    heads = range(DN_HEADS)
    qb, qgb, kbb, ktb, kdtb, rhs, decay = [], [], [], [], [], [], []
    for h in heads:
        q = conv_silu(q_ref, 0, h)
        k = conv_silu(k_ref, DN_WIDTH, h)
        v = conv_silu(v_ref, 2 * DN_WIDTH, h)
        q = q * (lax.rsqrt(jnp.sum(q * q, axis=-1, keepdims=True) + EPS) * (DN_HEAD_DIM ** -0.5))
        k = k * lax.rsqrt(jnp.sum(k * k, axis=-1, keepdims=True) + EPS)
        beta = beta_t[:, DN_HEADS + h:DN_HEADS + h + 1]
        egc = egc_t[:, h:h + 1]
        kb = k * beta
        kt = k.T
        qb.append(q.astype(BF16))
        qgb.append((q * egc).astype(BF16))
        kbb.append(kb.astype(BF16))
        ktb.append(kt.astype(BF16))
        kdtb.append((kt * edl8[h:h + 1, :]).astype(BF16))
        rhs.append(jnp.concatenate([v * beta, kb * egc], axis=1).astype(BF16))
        decay.append(jnp.exp(jnp.where(tri, gc[:, h:h + 1] - gct8[h:h + 1, :], -jnp.inf)))

    kk = [_dot(kbb[h], ktb[h]) for h in heads]
    qk = [_dot(qb[h], ktb[h]) for h in heads]
    a = [jnp.where(strict, kk[h] * decay[h], 0.0) for h in heads]
    aib = [(qk[h] * decay[h]).astype(BF16) for h in heads]
    t = _unit_lower_inverse(a, masks, eye)

    tv = [_dot(t[h].astype(BF16), rhs[h]) for h in heads]
    stb = [state_ref[h].astype(BF16) for h in heads]
    ks = [_dot(tv[h][:, LANES:].astype(BF16), stb[h]) for h in heads]
    o_inter = [_dot(qgb[h], stb[h]) for h in heads]
    vnb = [(tv[h][:, :LANES] - ks[h]).astype(BF16) for h in heads]
    o_intra = [_dot(aib[h], vnb[h]) for h in heads]
    upd = [_dot(kdtb[h], vnb[h]) for h in heads]
    for h in heads:
        cs = slice(h * LANES, (h + 1) * LANES)
        egl = jnp.exp(gct8[h:h + 1, C - 1:C])
        state_ref[h] = state_ref[h] * egl + upd[h]
        o = o_inter[h] + o_intra[h]
        on = o * lax.rsqrt(jnp.mean(o * o, axis=-1, keepdims=True) + EPS) * ng_ref[...]
        o_ref[:, cs] = (on * _silu(z_ref[:, cs].astype(F32))).astype(BF16)


def _deltanet(p, ab, conv_w, alog, dtb, ng):
    S = p.shape[0]
    C = DN_BLOCK
    blk = lambda j: pl.BlockSpec((C, DN_WIDTH), lambda m, j=j: (m, j))
    full = lambda shp: pl.BlockSpec(shp, lambda m: (0,) * len(shp))
    return pl.pallas_call(
        _dn_kernel,
        out_shape=jax.ShapeDtypeStruct((S, DN_WIDTH), BF16),
        grid=(S // C,),
        in_specs=[blk(COL_Q // 2), blk(COL_K // 2), blk(COL_DV // 2), blk(COL_DZ // 2),
                  pl.BlockSpec((C, LANES), lambda m: (m, 0)),
                  full((DN_CONV, 3 * DN_WIDTH)), full((1, LANES)), full((1, LANES)),
                  full((1, DN_HEAD_DIM))],
        out_specs=pl.BlockSpec((C, DN_WIDTH), lambda m: (m, 0)),
        scratch_shapes=[pltpu.VMEM((DN_HEADS, DN_HEAD_DIM, DN_HEAD_DIM), F32),
                        pltpu.VMEM((SUBLANES, 3 * DN_WIDTH), F32)],
        compiler_params=pltpu.CompilerParams(dimension_semantics=("arbitrary",),
                                             vmem_limit_bytes=VMEM_LIMIT),
        name="deltanet",
    )(p, p, p, p, ab, conv_w, alog, dtb, ng)


def _out_kernel(x_ref, a_ref, b_ref, c_ref, w_ref, fg_ref, y_ref):
    acc = _dot(a_ref[...], w_ref[0:GMLP_WIDTH, :])
    acc = acc + _dot(b_ref[...], w_ref[GMLP_WIDTH:GMLP_WIDTH + DN_WIDTH, :])
    acc = acc + _dot(c_ref[...], w_ref[GMLP_WIDTH + DN_WIDTH:, :])
    r = x_ref[...] + acc
    ms = jnp.mean(r * r, axis=-1, keepdims=True)
    y_ref[...] = r * lax.rsqrt(ms + EPS) * fg_ref[...]


def _outproj(x2, oa, ob, oc, w_out, fg):
    S = x2.shape[0]
    tm = min(512, S)
    row = lambda w: pl.BlockSpec((tm, w), lambda m: (m, 0))
    return pl.pallas_call(
        _out_kernel,
        out_shape=jax.ShapeDtypeStruct((S, D_MODEL), F32),
        grid=(S // tm,),
        in_specs=[row(D_MODEL), row(GMLP_WIDTH), row(DN_WIDTH), row(XA_WIDTH),
                  pl.BlockSpec((D_MODEL, D_MODEL), lambda m: (0, 0)),
                  pl.BlockSpec((1, D_MODEL), lambda m: (0, 0))],
        out_specs=row(D_MODEL),
        compiler_params=pltpu.CompilerParams(dimension_semantics=("arbitrary",),
                                             vmem_limit_bytes=VMEM_LIMIT),
        name="outproj",
    )(x2, oa, ob, oc, w_out, fg)


def _pad_lanes(v):
    return jnp.pad(v.astype(F32), (0, LANES - v.shape[0]))[None, :]


def kernel(x, mem, ln_g, w_in, gmlp_ln_g, gmlp_ln_b, gmlp_ws, gmlp_bs, conv_w, dn_a_log,
           dn_dt_bias, dn_norm_g, mem_norm_g, w_mem_kv, w_out, final_g):
    B, S, _ = x.shape
    assert B == 1 and ln_g.shape[0] == 1 and S % DN_BLOCK == 0
    x2 = x[0]
    w = w_in[0]
    ab0 = MAIN_WIDTH - 2 * XA_WIDTH
    w_c = w[:, ab0 + 2 * DN_HEADS:].astype(BF16)
    w_ab = jnp.pad(w[:, ab0:ab0 + 2 * DN_HEADS], ((0, 0), (0, LANES - 2 * DN_HEADS))).astype(BF16)

    p, ab = _proj(x2, ln_g, w, w_c, w_ab)
    out_a = _gmlp(p, gmlp_ln_g, gmlp_ln_b, gmlp_ws[0], gmlp_bs[0].T)
    kv = _memkv(mem[0], mem_norm_g, w_mem_kv[0].astype(BF16))
    out_c = _xattn(p, kv)
    out_b = _deltanet(p, ab, conv_w[0], _pad_lanes(dn_a_log[0]), _pad_lanes(dn_dt_bias[0]),
                      dn_norm_g)
    y = _outproj(x2, out_a, out_b, out_c, w_out[0].astype(BF16), final_g[None, :])
    return y[None]
```

```python
import functools

import jax
import jax.numpy as jnp
from jax import lax
from jax.experimental import pallas as pl
from jax.experimental.pallas import tpu as pltpu

F32 = jnp.float32
BF16 = jnp.bfloat16

D_MODEL = 2048
GMLP_CHUNK = 128
GMLP_GROUPS = 4
GMLP_WIDTH = 512
DN_HEADS = 8
DN_HEAD_DIM = 128
DN_WIDTH = 1024
DN_CONV = 4
XA_HEADS = 4
XA_HEAD_DIM = 128
XA_WIDTH = 512
EPS = 1e-6

LANES = 128
SUBLANES = 8
DN_BLOCK = 128
INV_BASE = 16

MAIN_WIDTH = 3 * GMLP_WIDTH + 4 * DN_WIDTH + 2 * XA_WIDTH
PROJ_TN = 512
PROJ_DIRECT_TILES = (3 * GMLP_WIDTH + 4 * DN_WIDTH) // PROJ_TN
COL_Q, COL_K, COL_DV, COL_DZ = 0, 2, 4, 6
COL_U, COL_V, COL_GZ = 8, 9, 10
COL_CQ, COL_CZ = 11, 12

VMEM_LIMIT = 48 * 1024 * 1024


def _dot(a, b):
    return jnp.dot(a, b, preferred_element_type=F32)


def _dot_nt(a, b):
    return lax.dot_general(a, b, (((1,), (1,)), ((), ())), preferred_element_type=F32)


def _silu(x):
    return x * jax.nn.sigmoid(x)


def _proj_kernel(x_ref, g_ref, wf_ref, wc_ref, wab_ref, p_ref, ab_ref, h_ref, *, row_chunk):
    n = pl.program_id(1)
    tm = x_ref.shape[0]

    @pl.when(n == 0)
    def _():
        def body(c, carry):
            r0 = pl.multiple_of(c * row_chunk, row_chunk)
            xf = x_ref[pl.ds(r0, row_chunk), :]
            ms = jnp.mean(xf * xf, axis=-1, keepdims=True)
            h_ref[pl.ds(r0, row_chunk), :] = (xf * lax.rsqrt(ms + EPS) * g_ref[...]).astype(BF16)
            return carry

        lax.fori_loop(0, tm // row_chunk, body, 0)
        ab_ref[...] = _dot_nt(h_ref[...], wab_ref[...].astype(BF16))

    @pl.when(n < PROJ_DIRECT_TILES)
    def _():
        p_ref[...] = _dot_nt(h_ref[...], wf_ref[...].astype(BF16)).astype(BF16)

    @pl.when(n >= PROJ_DIRECT_TILES)
    def _():
        p_ref[...] = _dot_nt(h_ref[...], wc_ref[...].astype(BF16)).astype(BF16)


def _proj_direct_block(n):
    dn_tiles = 4 * DN_WIDTH // PROJ_TN
    gm_tiles = 3 * GMLP_WIDTH // PROJ_TN
    last = PROJ_DIRECT_TILES - 1 - dn_tiles
    return jnp.where(n < dn_tiles, n + gm_tiles, jnp.minimum(n - dn_tiles, last))


def _proj(x2, ln_g, w_t, w_c, w_ab):
    S = x2.shape[0]
    tm = min(1024, S)
    grid = (S // tm, MAIN_WIDTH // PROJ_TN)
    return pl.pallas_call(
        functools.partial(_proj_kernel, row_chunk=128),
        out_shape=(jax.ShapeDtypeStruct((S, MAIN_WIDTH), BF16),
                   jax.ShapeDtypeStruct((S, LANES), F32)),
        grid=grid,
        in_specs=[
            pl.BlockSpec((tm, D_MODEL), lambda m, n: (m, 0)),
            pl.BlockSpec((1, D_MODEL), lambda m, n: (0, 0)),
            pl.BlockSpec((PROJ_TN, D_MODEL), lambda m, n: (_proj_direct_block(n), 0)),
            pl.BlockSpec((PROJ_TN, D_MODEL), lambda m, n: (jnp.maximum(n - PROJ_DIRECT_TILES, 0), 0)),
            pl.BlockSpec((LANES, D_MODEL), lambda m, n: (0, 0)),
        ],
        out_specs=(
            pl.BlockSpec((tm, PROJ_TN), lambda m, n: (m, n)),
            pl.BlockSpec((tm, LANES), lambda m, n: (m, 0)),
        ),
        scratch_shapes=[pltpu.VMEM((tm, D_MODEL), BF16)],
        compiler_params=pltpu.CompilerParams(
            dimension_semantics=("arbitrary", "arbitrary"),
            vmem_limit_bytes=VMEM_LIMIT),
        name="proj",
    )(x2, ln_g, w_t, w_c, w_ab)


def _gmlp_kernel(u_ref, v_ref, z_ref, lng_ref, lnb_ref, ws_ref, bst_ref, o_ref):
    tm = u_ref.shape[0]
    T = GMLP_CHUNK
    v = jax.nn.gelu(v_ref[...].astype(F32))
    mu = jnp.mean(v, axis=-1, keepdims=True)
    vc = v - mu
    var = jnp.mean(vc * vc, axis=-1, keepdims=True)
    vn = (vc * lax.rsqrt(var + EPS) * lng_ref[...] + lnb_ref[...]).astype(BF16)
    row = lax.broadcasted_iota(jnp.int32, (T, T), 0)
    col = lax.broadcasted_iota(jnp.int32, (T, T), 1)
    causal = row >= col
    for g in range(GMLP_GROUPS):
        w = jnp.where(causal, ws_ref[g], 0.0).astype(BF16)
        bias = bst_ref[:, g:g + 1]
        cs = slice(g * LANES, (g + 1) * LANES)
        for c in range(tm // T):
            rs = slice(c * T, (c + 1) * T)
            s = _dot(w, vn[rs, cs]) + bias
            u = jax.nn.gelu(u_ref[rs, cs].astype(F32))
            o_ref[rs, cs] = (u * s * _silu(z_ref[rs, cs].astype(F32))).astype(BF16)


def _gmlp(p, ln_g, ln_b, ws, bs_t):
    S = p.shape[0]
    tm = min(512, S)
    blk = lambda j: pl.BlockSpec((tm, GMLP_WIDTH), lambda m, j=j: (m, j))
    full = lambda shp: pl.BlockSpec(shp, lambda m: (0,) * len(shp))
    return pl.pallas_call(
        _gmlp_kernel,
        out_shape=jax.ShapeDtypeStruct((S, GMLP_WIDTH), BF16),
        grid=(S // tm,),
        in_specs=[blk(COL_U), blk(COL_V), blk(COL_GZ),
                  full((1, GMLP_WIDTH)), full((1, GMLP_WIDTH)),
                  full((GMLP_GROUPS, GMLP_CHUNK, GMLP_CHUNK)), full((GMLP_CHUNK, GMLP_GROUPS))],
        out_specs=pl.BlockSpec((tm, GMLP_WIDTH), lambda m: (m, 0)),
        compiler_params=pltpu.CompilerParams(dimension_semantics=("arbitrary",),
                                             vmem_limit_bytes=VMEM_LIMIT),
        name="gmlp",
    )(p, p, p, ln_g, ln_b, ws, bs_t)


def _memkv_kernel(mem_ref, g_ref, w_ref, kv_ref):
    mf = mem_ref[...]
    ms = jnp.mean(mf * mf, axis=-1, keepdims=True)
    m = (mf * lax.rsqrt(ms + EPS) * g_ref[...]).astype(BF16)
    kv_ref[...] = _dot(m, w_ref[...]).astype(BF16)


def _memkv(mem2, g, w):
    M = mem2.shape[0]
    return pl.pallas_call(
        _memkv_kernel,
        out_shape=jax.ShapeDtypeStruct((M, 2 * XA_WIDTH), BF16),
        compiler_params=pltpu.CompilerParams(vmem_limit_bytes=VMEM_LIMIT),
        name="memkv",
    )(mem2, g, w)


def _xattn_kernel(q_ref, z_ref, kv_ref, o_ref):
    scale = XA_HEAD_DIM ** -0.5
    for h in range(XA_HEADS):
        cs = slice(h * LANES, (h + 1) * LANES)
        mk = kv_ref[:, h * LANES:(h + 1) * LANES]
        mv = kv_ref[:, XA_WIDTH + h * LANES:XA_WIDTH + (h + 1) * LANES]
        s = lax.dot_general(q_ref[:, cs], mk, (((1,), (1,)), ((), ())),
                            preferred_element_type=F32) * scale
        mx = jnp.max(s, axis=-1, keepdims=True)
        e = jnp.exp(s - mx)
        l = jnp.sum(e, axis=-1, keepdims=True)
        o = _dot(e.astype(BF16), mv) / l
        o_ref[:, cs] = (o * _silu(z_ref[:, cs].astype(F32))).astype(BF16)


def _xattn(p, kv):
    S = p.shape[0]
    M = kv.shape[0]
    tm = min(512, S)
    return pl.pallas_call(
        _xattn_kernel,
        out_shape=jax.ShapeDtypeStruct((S, XA_WIDTH), BF16),
        grid=(S // tm,),
        in_specs=[pl.BlockSpec((tm, XA_WIDTH), lambda m: (m, COL_CQ)),
                  pl.BlockSpec((tm, XA_WIDTH), lambda m: (m, COL_CZ)),
                  pl.BlockSpec((M, 2 * XA_WIDTH), lambda m: (0, 0))],
        out_specs=pl.BlockSpec((tm, XA_WIDTH), lambda m: (m, 0)),
        compiler_params=pltpu.CompilerParams(dimension_semantics=("arbitrary",),
                                             vmem_limit_bytes=VMEM_LIMIT),
        name="xattn",
    )(p, p, kv)


def _unit_lower_inverse(a_list, masks, eye):
    base_mask, merge_masks = masks
    n_mat = range(len(a_list))
    x = [jnp.where(base_mask, -a, 0.0) for a in a_list]
    xb = [v.astype(BF16) for v in x]
    t = [eye + v for v in x]
    n = 2
    while n < INV_BASE:
        x = [_dot(xb[j], xb[j]) for j in n_mat]
        xb = [v.astype(BF16) for v in x]
        tx = [_dot(t[j].astype(BF16), xb[j]) for j in n_mat]
        t = [t[j] + tx[j] for j in n_mat]
        n *= 2
    for m in merge_masks:
        al = [jnp.where(m, a, 0.0).astype(BF16) for a in a_list]
        tb = [v.astype(BF16) for v in t]
        u = [_dot(tb[j], al[j]).astype(BF16) for j in n_mat]
        w = [_dot(u[j], tb[j]) for j in n_mat]
        t = [t[j] - w[j] for j in n_mat]
    return t


def _dn_kernel(q_ref, k_ref, v_ref, z_ref, ab_ref, cw_ref, alog_ref, dtb_ref, ng_ref,
               o_ref, state_ref, tail_ref):
    C = DN_BLOCK
    i = pl.program_id(0)

    @pl.when(i == 0)
    def _():
        state_ref[...] = jnp.zeros_like(state_ref)
        tail_ref[...] = jnp.zeros_like(tail_ref)

    row = lax.broadcasted_iota(jnp.int32, (C, C), 0)
    col = lax.broadcasted_iota(jnp.int32, (C, C), 1)
    tri = row >= col
    strict = row > col
    eye = (row == col).astype(F32)
    same = lambda b: (row // b) == (col // b)
    base_mask = same(INV_BASE)
    merge_masks = []
    b = INV_BASE
    while b < C:
        merge_masks.append(jnp.logical_and(same(2 * b), jnp.logical_not(same(b))))
        b *= 2
    masks = (base_mask, merge_masks)
    row8 = lax.broadcasted_iota(jnp.int32, (SUBLANES, LANES), 0)

    ab = ab_ref[...]
    lane = lax.broadcasted_iota(jnp.int32, (C, LANES), 1)
    xa = ab + dtb_ref[...]
    softplus = jnp.maximum(xa, 0.0) + jnp.log1p(jnp.exp(-jnp.abs(xa)))
    g = jnp.where(lane < DN_HEADS, -jnp.exp(alog_ref[...]) * softplus, 0.0)
    beta_t = jax.nn.sigmoid(ab)

    ones_l = tri.astype(BF16)
    g1 = g.astype(BF16)
    r1 = g - g1.astype(F32)
    g2 = r1.astype(BF16)
    g3 = (r1 - g2.astype(F32)).astype(BF16)
    gc = _dot(ones_l, g1) + _dot(ones_l, g2) + _dot(ones_l, g3)
    egc_t = jnp.exp(gc)
    gct8 = gc.T[0:SUBLANES, :]
    edl8 = jnp.exp(gct8[:, C - 1:C] - gct8)

    def conv_silu(ref, c0, h):
        cs = slice(h * LANES, (h + 1) * LANES)
        ts = slice(c0 + h * LANES, c0 + (h + 1) * LANES)
        cur = ref[:, cs].astype(F32)
        tail = tail_ref[:, ts]
        acc = cur * cw_ref[DN_CONV - 1:DN_CONV, ts]
        for s in range(1, DN_CONV):
            r = pltpu.roll(cur, s, 0)
            rt = pltpu.roll(tail, s, 0)
            first = jnp.where(row8 < s, rt, r[0:SUBLANES])
            sh = jnp.concatenate([first, r[SUBLANES:]], axis=0)
            acc = acc + sh * cw_ref[DN_CONV - 1 - s:DN_CONV - s, ts]
        tail_ref[:, ts] = cur[C - SUBLANES:C]
        return _silu(acc)

    heads = range(DN_HEADS)
    qb, qgb, kbb, ktb, kdtb, rhs, decay = [], [], [], [], [], [], []
    for h in heads:
        q = conv_silu(q_ref, 0, h)
        k = conv_silu(k_ref, DN_WIDTH, h)
        v = conv_silu(v_ref, 2 * DN_WIDTH, h)
        q = q * (lax.rsqrt(jnp.sum(q * q, axis=-1, keepdims=True) + EPS) * (DN_HEAD_DIM ** -0.5))
        k = k * lax.rsqrt(jnp.sum(k * k, axis=-1, keepdims=True) + EPS)
        beta = beta_t[:, DN_HEADS + h:DN_HEADS + h + 1]
        egc = egc_t[:, h:h + 1]
        kb = k * beta
        kt = k.T
        qb.append(q.astype(BF16))
        qgb.append((q * egc).astype(BF16))
        kbb.append(kb.astype(BF16))
        ktb.append(kt.astype(BF16))
        kdtb.append((kt * edl8[h:h + 1, :]).astype(BF16))
        rhs.append(jnp.concatenate([v * beta, kb * egc], axis=1).astype(BF16))
        decay.append(jnp.exp(jnp.where(tri, gc[:, h:h + 1] - gct8[h:h + 1, :], -jnp.inf)))

    kk = [_dot(kbb[h], ktb[h]) for h in heads]
    qk = [_dot(qb[h], ktb[h]) for h in heads]
    a = [jnp.where(strict, kk[h] * decay[h], 0.0) for h in heads]
    aib = [(qk[h] * decay[h]).astype(BF16) for h in heads]
    t = _unit_lower_inverse(a, masks, eye)

    tv = [_dot(t[h].astype(BF16), rhs[h]) for h in heads]
    stb = [state_ref[h].astype(BF16) for h in heads]
    ks = [_dot(tv[h][:, LANES:].astype(BF16), stb[h]) for h in heads]
    o_inter = [_dot(qgb[h], stb[h]) for h in heads]
    vnb = [(tv[h][:, :LANES] - ks[h]).astype(BF16) for h in heads]
    o_intra = [_dot(aib[h], vnb[h]) for h in heads]
    upd = [_dot(kdtb[h], vnb[h]) for h in heads]
    for h in heads:
        cs = slice(h * LANES, (h + 1) * LANES)
        egl = jnp.exp(gct8[h:h + 1, C - 1:C])
        state_ref[h] = state_ref[h] * egl + upd[h]
        o = o_inter[h] + o_intra[h]
        on = o * lax.rsqrt(jnp.mean(o * o, axis=-1, keepdims=True) + EPS) * ng_ref[...]
        o_ref[:, cs] = (on * _silu(z_ref[:, cs].astype(F32))).astype(BF16)


def _deltanet(p, ab, conv_w, alog, dtb, ng):
    S = p.shape[0]
    C = DN_BLOCK
    blk = lambda j: pl.BlockSpec((C, DN_WIDTH), lambda m, j=j: (m, j))
    full = lambda shp: pl.BlockSpec(shp, lambda m: (0,) * len(shp))
    return pl.pallas_call(
        _dn_kernel,
        out_shape=jax.ShapeDtypeStruct((S, DN_WIDTH), BF16),
        grid=(S // C,),
        in_specs=[blk(COL_Q // 2), blk(COL_K // 2), blk(COL_DV // 2), blk(COL_DZ // 2),
                  pl.BlockSpec((C, LANES), lambda m: (m, 0)),
                  full((DN_CONV, 3 * DN_WIDTH)), full((1, LANES)), full((1, LANES)),
                  full((1, DN_HEAD_DIM))],
        out_specs=pl.BlockSpec((C, DN_WIDTH), lambda m: (m, 0)),
        scratch_shapes=[pltpu.VMEM((DN_HEADS, DN_HEAD_DIM, DN_HEAD_DIM), F32),
                        pltpu.VMEM((SUBLANES, 3 * DN_WIDTH), F32)],
        compiler_params=pltpu.CompilerParams(dimension_semantics=("arbitrary",),
                                             vmem_limit_bytes=VMEM_LIMIT),
        name="deltanet",
    )(p, p, p, p, ab, conv_w, alog, dtb, ng)


def _out_kernel(x_ref, a_ref, b_ref, c_ref, w_ref, fg_ref, y_ref):
    acc = _dot(a_ref[...], w_ref[0:GMLP_WIDTH, :])
    acc = acc + _dot(b_ref[...], w_ref[GMLP_WIDTH:GMLP_WIDTH + DN_WIDTH, :])
    acc = acc + _dot(c_ref[...], w_ref[GMLP_WIDTH + DN_WIDTH:, :])
    r = x_ref[...] + acc
    ms = jnp.mean(r * r, axis=-1, keepdims=True)
    y_ref[...] = r * lax.rsqrt(ms + EPS) * fg_ref[...]


def _outproj(x2, oa, ob, oc, w_out, fg):
    S = x2.shape[0]
    tm = min(512, S)
    row = lambda w: pl.BlockSpec((tm, w), lambda m: (m, 0))
    return pl.pallas_call(
        _out_kernel,
        out_shape=jax.ShapeDtypeStruct((S, D_MODEL), F32),
        grid=(S // tm,),
        in_specs=[row(D_MODEL), row(GMLP_WIDTH), row(DN_WIDTH), row(XA_WIDTH),
                  pl.BlockSpec((D_MODEL, D_MODEL), lambda m: (0, 0)),
                  pl.BlockSpec((1, D_MODEL), lambda m: (0, 0))],
        out_specs=row(D_MODEL),
        compiler_params=pltpu.CompilerParams(dimension_semantics=("arbitrary",),
                                             vmem_limit_bytes=VMEM_LIMIT),
        name="outproj",
    )(x2, oa, ob, oc, w_out, fg)


def _pad_lanes(v):
    return jnp.pad(v.astype(F32), (0, LANES - v.shape[0]))[None, :]


def kernel(x, mem, ln_g, w_in, gmlp_ln_g, gmlp_ln_b, gmlp_ws, gmlp_bs, conv_w, dn_a_log,
           dn_dt_bias, dn_norm_g, mem_norm_g, w_mem_kv, w_out, final_g):
    B, S, _ = x.shape
    assert B == 1 and ln_g.shape[0] == 1 and S % DN_BLOCK == 0
    x2 = x[0]
    w_t = jnp.swapaxes(w_in, 1, 2)[0]
    ab0 = MAIN_WIDTH - 2 * XA_WIDTH
    w_c = w_t[ab0 + 2 * DN_HEADS:]
    w_ab = jnp.pad(w_t[ab0:ab0 + 2 * DN_HEADS], ((0, LANES - 2 * DN_HEADS), (0, 0)))

    p, ab = _proj(x2, ln_g, w_t, w_c, w_ab)
    out_a = _gmlp(p, gmlp_ln_g, gmlp_ln_b, gmlp_ws[0], gmlp_bs[0].T)
    kv = _memkv(mem[0], mem_norm_g, w_mem_kv[0].astype(BF16))
    out_c = _xattn(p, kv)
    out_b = _deltanet(p, ab, conv_w[0], _pad_lanes(dn_a_log[0]), _pad_lanes(dn_dt_bias[0]),
                      dn_norm_g)
    y = _outproj(x2, out_a, out_b, out_c, w_out[0].astype(BF16), final_g[None, :])
    return y[None]
```

```python
import functools

import jax
import jax.numpy as jnp
from jax import lax
from jax.experimental import pallas as pl
from jax.experimental.pallas import tpu as pltpu

F32 = jnp.float32
BF16 = jnp.bfloat16

D_MODEL = 2048
GMLP_CHUNK = 128
GMLP_GROUPS = 4
GMLP_WIDTH = 512
DN_HEADS = 8
DN_HEAD_DIM = 128
DN_WIDTH = 1024
DN_CONV = 4
XA_HEADS = 4
XA_HEAD_DIM = 128
XA_WIDTH = 512
EPS = 1e-6

LANES = 128
SUBLANES = 8
DN_BLOCK = 128
INV_BASE = 16

MAIN_WIDTH = 3 * GMLP_WIDTH + 4 * DN_WIDTH + 2 * XA_WIDTH
PROJ_TN = 512
PROJ_DIRECT_TILES = (3 * GMLP_WIDTH + 4 * DN_WIDTH) // PROJ_TN
COL_Q, COL_K, COL_DV, COL_DZ = 0, 2, 4, 6
COL_U, COL_V, COL_GZ = 8, 9, 10
COL_CQ, COL_CZ = 11, 12

VMEM_LIMIT = 48 * 1024 * 1024


def _dot(a, b):
    return jnp.dot(a, b, preferred_element_type=F32)


def _dot_nt(a, b):
    return lax.dot_general(a, b, (((1,), (1,)), ((), ())), preferred_element_type=F32)


def _silu(x):
    return x * jax.nn.sigmoid(x)


def _proj_kernel(x_ref, g_ref, wf_ref, wc_ref, wab_ref, p_ref, ab_ref, h_ref, *, row_chunk):
    n = pl.program_id(1)
    tm = x_ref.shape[0]

    @pl.when(n == 0)
    def _():
        def body(c, carry):
            r0 = pl.multiple_of(c * row_chunk, row_chunk)
            xf = x_ref[pl.ds(r0, row_chunk), :]
            ms = jnp.mean(xf * xf, axis=-1, keepdims=True)
            h_ref[pl.ds(r0, row_chunk), :] = (xf * lax.rsqrt(ms + EPS) * g_ref[...]).astype(BF16)
            return carry

        lax.fori_loop(0, tm // row_chunk, body, 0)
        ab_ref[...] = _dot_nt(h_ref[...], wab_ref[...].astype(BF16))

    @pl.when(n < PROJ_DIRECT_TILES)
    def _():
        p_ref[...] = _dot_nt(h_ref[...], wf_ref[...].astype(BF16)).astype(BF16)

    @pl.when(n >= PROJ_DIRECT_TILES)
    def _():
        p_ref[...] = _dot_nt(h_ref[...], wc_ref[...].astype(BF16)).astype(BF16)


def _proj_direct_block(n):
    dn_tiles = 4 * DN_WIDTH // PROJ_TN
    gm_tiles = 3 * GMLP_WIDTH // PROJ_TN
    last = PROJ_DIRECT_TILES - 1 - dn_tiles
    return jnp.where(n < dn_tiles, n + gm_tiles, jnp.minimum(n - dn_tiles, last))


def _proj(x2, ln_g, w_t, w_c, w_ab):
    S = x2.shape[0]
    tm = min(1024, S)
    grid = (S // tm, MAIN_WIDTH // PROJ_TN)
    return pl.pallas_call(
        functools.partial(_proj_kernel, row_chunk=128),
        out_shape=(jax.ShapeDtypeStruct((S, MAIN_WIDTH), BF16),
                   jax.ShapeDtypeStruct((S, LANES), F32)),
        grid=grid,
        in_specs=[
            pl.BlockSpec((tm, D_MODEL), lambda m, n: (m, 0)),
            pl.BlockSpec((1, D_MODEL), lambda m, n: (0, 0)),
            pl.BlockSpec((PROJ_TN, D_MODEL), lambda m, n: (_proj_direct_block(n), 0)),
            pl.BlockSpec((PROJ_TN, D_MODEL), lambda m, n: (jnp.maximum(n - PROJ_DIRECT_TILES, 0), 0)),
            pl.BlockSpec((LANES, D_MODEL), lambda m, n: (0, 0)),
        ],
        out_specs=(
            pl.BlockSpec((tm, PROJ_TN), lambda m, n: (m, n)),
            pl.BlockSpec((tm, LANES), lambda m, n: (m, 0)),
        ),
        scratch_shapes=[pltpu.VMEM((tm, D_MODEL), BF16)],
        compiler_params=pltpu.CompilerParams(
            dimension_semantics=("arbitrary", "arbitrary"),
            vmem_limit_bytes=VMEM_LIMIT),
        name="proj",
    )(x2, ln_g, w_t, w_c, w_ab)


def _gmlp_kernel(u_ref, v_ref, z_ref, lng_ref, lnb_ref, ws_ref, bst_ref, o_ref):
    tm = u_ref.shape[0]
    T = GMLP_CHUNK
    v = jax.nn.gelu(v_ref[...].astype(F32))
    mu = jnp.mean(v, axis=-1, keepdims=True)
    vc = v - mu
    var = jnp.mean(vc * vc, axis=-1, keepdims=True)
    vn = (vc * lax.rsqrt(var + EPS) * lng_ref[...] + lnb_ref[...]).astype(BF16)
    row = lax.broadcasted_iota(jnp.int32, (T, T), 0)
    col = lax.broadcasted_iota(jnp.int32, (T, T), 1)
    causal = row >= col
    for g in range(GMLP_GROUPS):
        w = jnp.where(causal, ws_ref[g], 0.0).astype(BF16)
        bias = bst_ref[:, g:g + 1]
        cs = slice(g * LANES, (g + 1) * LANES)
        for c in range(tm // T):
            rs = slice(c * T, (c + 1) * T)
            s = _dot(w, vn[rs, cs]) + bias
            u = jax.nn.gelu(u_ref[rs, cs].astype(F32))
            o_ref[rs, cs] = (u * s * _silu(z_ref[rs, cs].astype(F32))).astype(BF16)


def _gmlp(p, ln_g, ln_b, ws, bs_t):
    S = p.shape[0]
    tm = min(512, S)
    blk = lambda j: pl.BlockSpec((tm, GMLP_WIDTH), lambda m, j=j: (m, j))
    full = lambda shp: pl.BlockSpec(shp, lambda m: (0,) * len(shp))
    return pl.pallas_call(
        _gmlp_kernel,
        out_shape=jax.ShapeDtypeStruct((S, GMLP_WIDTH), BF16),
        grid=(S // tm,),
        in_specs=[blk(COL_U), blk(COL_V), blk(COL_GZ),
                  full((1, GMLP_WIDTH)), full((1, GMLP_WIDTH)),
                  full((GMLP_GROUPS, GMLP_CHUNK, GMLP_CHUNK)), full((GMLP_CHUNK, GMLP_GROUPS))],
        out_specs=pl.BlockSpec((tm, GMLP_WIDTH), lambda m: (m, 0)),
        compiler_params=pltpu.CompilerParams(dimension_semantics=("arbitrary",),
                                             vmem_limit_bytes=VMEM_LIMIT),
        name="gmlp",
    )(p, p, p, ln_g, ln_b, ws, bs_t)


def _memkv_kernel(mem_ref, g_ref, w_ref, kv_ref):
    mf = mem_ref[...]
    ms = jnp.mean(mf * mf, axis=-1, keepdims=True)
    m = (mf * lax.rsqrt(ms + EPS) * g_ref[...]).astype(BF16)
    kv_ref[...] = _dot(m, w_ref[...]).astype(BF16)


def _memkv(mem2, g, w):
    M = mem2.shape[0]
    return pl.pallas_call(
        _memkv_kernel,
        out_shape=jax.ShapeDtypeStruct((M, 2 * XA_WIDTH), BF16),
        compiler_params=pltpu.CompilerParams(vmem_limit_bytes=VMEM_LIMIT),
        name="memkv",
    )(mem2, g, w)


def _xattn_kernel(q_ref, z_ref, kv_ref, o_ref):
    scale = XA_HEAD_DIM ** -0.5
    for h in range(XA_HEADS):
        cs = slice(h * LANES, (h + 1) * LANES)
        mk = kv_ref[:, h * LANES:(h + 1) * LANES]
        mv = kv_ref[:, XA_WIDTH + h * LANES:XA_WIDTH + (h + 1) * LANES]
        s = _dot_nt(q_ref[:, cs], mk) * scale
        mx = jnp.max(s, axis=-1, keepdims=True)
        e = jnp.exp(s - mx)
        l = jnp.sum(e, axis=-1, keepdims=True)
        o = _dot(e.astype(BF16), mv) / l
        o_ref[:, cs] = (o * _silu(z_ref[:, cs].astype(F32))).astype(BF16)


def _xattn(p, kv):
    S = p.shape[0]
    M = kv.shape[0]
    tm = min(512, S)
    return pl.pallas_call(
        _xattn_kernel,
        out_shape=jax.ShapeDtypeStruct((S, XA_WIDTH), BF16),
        grid=(S // tm,),
        in_specs=[pl.BlockSpec((tm, XA_WIDTH), lambda m: (m, COL_CQ)),
                  pl.BlockSpec((tm, XA_WIDTH), lambda m: (m, COL_CZ)),
                  pl.BlockSpec((M, 2 * XA_WIDTH), lambda m: (0, 0))],
        out_specs=pl.BlockSpec((tm, XA_WIDTH), lambda m: (m, 0)),
        compiler_params=pltpu.CompilerParams(dimension_semantics=("arbitrary",),
                                             vmem_limit_bytes=VMEM_LIMIT),
        name="xattn",
    )(p, p, kv)


def _unit_lower_inverse(a_list, masks, eye):
    base_mask, merge_masks = masks
    n_mat = range(len(a_list))
    x = [jnp.where(base_mask, -a, 0.0) for a in a_list]
    xb = [v.astype(BF16) for v in x]
    t = [eye + v for v in x]
    n = 2
    while n < INV_BASE:
        x = [_dot(xb[j], xb[j]) for j in n_mat]
        xb = [v.astype(BF16) for v in x]
        tx = [_dot(t[j].astype(BF16), xb[j]) for j in n_mat]
        t = [t[j] + tx[j] for j in n_mat]
        n *= 2
    for m in merge_masks:
        al = [jnp.where(m, a, 0.0).astype(BF16) for a in a_list]
        tb = [v.astype(BF16) for v in t]
        u = [_dot(tb[j], al[j]).astype(BF16) for j in n_mat]
        w = [_dot(u[j], tb[j]) for j in n_mat]
        t = [t[j] - w[j] for j in n_mat]
    return t


def _dn_masks():
    C = DN_BLOCK
    row = lax.broadcasted_iota(jnp.int32, (C, C), 0)
    col = lax.broadcasted_iota(jnp.int32, (C, C), 1)
    return row, col


def _dn_stage(q_ref, k_ref, v_ref, xe_ref):
    C = DN_BLOCK
    for j, ref in enumerate((q_ref, k_ref, v_ref)):
        for h in range(DN_HEADS):
            t = j * DN_HEADS + h
            xe_ref[t, 0:SUBLANES, :] = xe_ref[t, C:C + SUBLANES, :]
            xe_ref[t, SUBLANES:SUBLANES + C, :] = ref[:, h * LANES:(h + 1) * LANES].astype(F32)


def _dn_prep(ab_ref, cw_ref, alog_ref, dtb_ref, xe_ref, slot):
    C = DN_BLOCK
    qb_s, qgb_s, kbb_s, ktb_s, kdtb_s, rhs_s, decay_s, g8_s = slot
    row, col = _dn_masks()
    tri = row >= col

    ab = ab_ref[...]
    lane = lax.broadcasted_iota(jnp.int32, (C, LANES), 1)
    xa = ab + dtb_ref[...]
    softplus = jnp.maximum(xa, 0.0) + jnp.log1p(jnp.exp(-jnp.abs(xa)))
    g = jnp.where(lane < DN_HEADS, -jnp.exp(alog_ref[...]) * softplus, 0.0)
    beta_t = jax.nn.sigmoid(ab)

    ones_l = tri.astype(BF16)
    g1 = g.astype(BF16)
    r1 = g - g1.astype(F32)
    g2 = r1.astype(BF16)
    g3 = (r1 - g2.astype(F32)).astype(BF16)
    gc = _dot(ones_l, g1) + _dot(ones_l, g2) + _dot(ones_l, g3)
    egc_t = jnp.exp(gc)
    gct8 = gc.T[0:SUBLANES, :]
    edl8 = jnp.exp(gct8[:, C - 1:C] - gct8)
    g8_s[...] = gct8

    def conv_silu(c0, h):
        t = (c0 // DN_WIDTH) * DN_HEADS + h
        ts = slice(c0 + h * LANES, c0 + (h + 1) * LANES)
        acc = xe_ref[t, SUBLANES:SUBLANES + C, :] * cw_ref[DN_CONV - 1:DN_CONV, ts]
        for s in range(1, DN_CONV):
            acc = acc + xe_ref[t, SUBLANES - s:SUBLANES - s + C, :] * cw_ref[DN_CONV - 1 - s:DN_CONV - s, ts]
        return _silu(acc)

    for h in range(DN_HEADS):
        q = conv_silu(0, h)
        k = conv_silu(DN_WIDTH, h)
        v = conv_silu(2 * DN_WIDTH, h)
        q = q * (lax.rsqrt(jnp.sum(q * q, axis=-1, keepdims=True) + EPS) * (DN_HEAD_DIM ** -0.5))
        k = k * lax.rsqrt(jnp.sum(k * k, axis=-1, keepdims=True) + EPS)
        beta = beta_t[:, DN_HEADS + h:DN_HEADS + h + 1]
        egc = egc_t[:, h:h + 1]
        kb = k * beta
        kt = k.T
        qb_s[h] = q.astype(BF16)
        qgb_s[h] = (q * egc).astype(BF16)
        kbb_s[h] = kb.astype(BF16)
        ktb_s[h] = kt.astype(BF16)
        kdtb_s[h] = (kt * edl8[h:h + 1, :]).astype(BF16)
        rhs_s[h] = jnp.concatenate([v * beta, kb * egc], axis=1).astype(BF16)
        decay_s[h] = jnp.exp(jnp.where(tri, gc[:, h:h + 1] - gct8[h:h + 1, :], -jnp.inf))


def _dn_chain(z_ref, ng_ref, o_ref, state_ref, slot):
    C = DN_BLOCK
    heads = range(DN_HEADS)
    qb_s, qgb_s, kbb_s, ktb_s, kdtb_s, rhs_s, decay_s, g8_s = slot
    row, col = _dn_masks()
    strict = row > col
    eye = (row == col).astype(F32)
    same = lambda b: (row // b) == (col // b)
    merge_masks = []
    b = INV_BASE
    while b < C:
        merge_masks.append(jnp.logical_and(same(2 * b), jnp.logical_not(same(b))))
        b *= 2
    masks = (same(INV_BASE), merge_masks)

    kk = [_dot(kbb_s[h], ktb_s[h]) for h in heads]
    qk = [_dot(qb_s[h], ktb_s[h]) for h in heads]
    a = [jnp.where(strict, kk[h] * decay_s[h], 0.0) for h in heads]
    aib = [(qk[h] * decay_s[h]).astype(BF16) for h in heads]
    t = _unit_lower_inverse(a, masks, eye)

    tv = [_dot(t[h].astype(BF16), rhs_s[h]) for h in heads]
    stb = [state_ref[h].astype(BF16) for h in heads]
    ks = [_dot(tv[h][:, LANES:].astype(BF16), stb[h]) for h in heads]
    o_inter = [_dot(qgb_s[h], stb[h]) for h in heads]
    vnb = [(tv[h][:, :LANES] - ks[h]).astype(BF16) for h in heads]
    o_intra = [_dot(aib[h], vnb[h]) for h in heads]
    upd = [_dot(kdtb_s[h], vnb[h]) for h in heads]
    for h in heads:
        cs = slice(h * LANES, (h + 1) * LANES)
        egl = jnp.exp(g8_s[h:h + 1, C - 1:C])
        state_ref[h] = state_ref[h] * egl + upd[h]
        o = o_inter[h] + o_intra[h]
        on = o * lax.rsqrt(jnp.mean(o * o, axis=-1, keepdims=True) + EPS) * ng_ref[...]
        o_ref[:, cs] = (on * _silu(z_ref[:, cs].astype(F32))).astype(BF16)


def _dn_kernel(q_ref, k_ref, v_ref, z_ref, ab_ref, cw_ref, alog_ref, dtb_ref, ng_ref,
               o_ref, state_ref, xe_ref, *slots):
    n_slot_refs = len(slots) // 2
    slot_a, slot_b = slots[:n_slot_refs], slots[n_slot_refs:]
    j = pl.program_id(0)

    @pl.when(j == 0)
    def _():
        state_ref[...] = jnp.zeros_like(state_ref)
        xe_ref[...] = jnp.zeros_like(xe_ref)
        for ref in slot_a:
            ref[...] = jnp.zeros_like(ref)

    def step(prep_slot, chain_slot):
        _dn_prep(ab_ref, cw_ref, alog_ref, dtb_ref, xe_ref, prep_slot)
        _dn_chain(z_ref, ng_ref, o_ref, state_ref, chain_slot)
        _dn_stage(q_ref, k_ref, v_ref, xe_ref)

    @pl.when(j % 2 == 0)
    def _():
        step(slot_b, slot_a)

    @pl.when(j % 2 == 1)
    def _():
        step(slot_a, slot_b)


def _deltanet(p, ab, conv_w, alog, dtb, ng):
    S = p.shape[0]
    C = DN_BLOCK
    nb = S // C
    clip = lambda b: jnp.clip(b, 0, nb - 1)
    blk = lambda c: pl.BlockSpec((C, DN_WIDTH), lambda j, c=c: (clip(j), c))
    full = lambda shp: pl.BlockSpec(shp, lambda j: (0,) * len(shp))
    hb = lambda w, dt: pltpu.VMEM((DN_HEADS, DN_HEAD_DIM, w), dt)
    slot = [hb(DN_HEAD_DIM, BF16)] * 5 + [hb(2 * DN_HEAD_DIM, BF16), hb(DN_HEAD_DIM, F32),
                                           pltpu.VMEM((SUBLANES, C), F32)]
    return pl.pallas_call(
        _dn_kernel,
        out_shape=jax.ShapeDtypeStruct((S, DN_WIDTH), BF16),
        grid=(nb + 2,),
        in_specs=[blk(COL_Q // 2), blk(COL_K // 2), blk(COL_DV // 2),
                  pl.BlockSpec((C, DN_WIDTH), lambda j: (clip(j - 2), COL_DZ // 2)),
                  pl.BlockSpec((C, LANES), lambda j: (clip(j - 1), 0)),
                  full((DN_CONV, 3 * DN_WIDTH)), full((1, LANES)), full((1, LANES)),
                  full((1, DN_HEAD_DIM))],
        out_specs=pl.BlockSpec((C, DN_WIDTH), lambda j: (clip(j - 2), 0)),
        scratch_shapes=[pltpu.VMEM((DN_HEADS, DN_HEAD_DIM, DN_HEAD_DIM), F32),
                        pltpu.VMEM((3 * DN_HEADS, SUBLANES + C, LANES), F32)] + slot + slot,
        compiler_params=pltpu.CompilerParams(dimension_semantics=("arbitrary",),
                                             vmem_limit_bytes=VMEM_LIMIT),
        name="deltanet",
    )(p, p, p, p, ab, conv_w, alog, dtb, ng)


def _out_kernel(x_ref, a_ref, b_ref, c_ref, w_ref, fg_ref, y_ref):
    acc = _dot(a_ref[...], w_ref[0:GMLP_WIDTH, :])
    acc = acc + _dot(b_ref[...], w_ref[GMLP_WIDTH:GMLP_WIDTH + DN_WIDTH, :])
    acc = acc + _dot(c_ref[...], w_ref[GMLP_WIDTH + DN_WIDTH:, :])
    r = x_ref[...] + acc
    ms = jnp.mean(r * r, axis=-1, keepdims=True)
    y_ref[...] = r * lax.rsqrt(ms + EPS) * fg_ref[...]


def _outproj(x2, oa, ob, oc, w_out, fg):
    S = x2.shape[0]
    tm = min(512, S)
    row = lambda w: pl.BlockSpec((tm, w), lambda m: (m, 0))
    return pl.pallas_call(
        _out_kernel,
        out_shape=jax.ShapeDtypeStruct((S, D_MODEL), F32),
        grid=(S // tm,),
        in_specs=[row(D_MODEL), row(GMLP_WIDTH), row(DN_WIDTH), row(XA_WIDTH),
                  pl.BlockSpec((D_MODEL, D_MODEL), lambda m: (0, 0)),
                  pl.BlockSpec((1, D_MODEL), lambda m: (0, 0))],
        out_specs=row(D_MODEL),
        compiler_params=pltpu.CompilerParams(dimension_semantics=("arbitrary",),
                                             vmem_limit_bytes=VMEM_LIMIT),
        name="outproj",
    )(x2, oa, ob, oc, w_out, fg)


def _pad_lanes(v):
    return jnp.pad(v.astype(F32), (0, LANES - v.shape[0]))[None, :]


def kernel(x, mem, ln_g, w_in, gmlp_ln_g, gmlp_ln_b, gmlp_ws, gmlp_bs, conv_w, dn_a_log,
           dn_dt_bias, dn_norm_g, mem_norm_g, w_mem_kv, w_out, final_g):
    B, S, _ = x.shape
    assert B == 1 and ln_g.shape[0] == 1 and S % DN_BLOCK == 0
    x2 = x[0]
    w_t = jnp.swapaxes(w_in, 1, 2)[0]
    ab0 = MAIN_WIDTH - 2 * XA_WIDTH
    w_c = w_t[ab0 + 2 * DN_HEADS:]
    w_ab = jnp.pad(w_t[ab0:ab0 + 2 * DN_HEADS], ((0, LANES - 2 * DN_HEADS), (0, 0)))

    p, ab = _proj(x2, ln_g, w_t, w_c, w_ab)
    out_a = _gmlp(p, gmlp_ln_g, gmlp_ln_b, gmlp_ws[0], gmlp_bs[0].T)
    kv = _memkv(mem[0], mem_norm_g, w_mem_kv[0].astype(BF16))
    out_c = _xattn(p, kv)
    out_b = _deltanet(p, ab, conv_w[0], _pad_lanes(dn_a_log[0]), _pad_lanes(dn_dt_bias[0]),
                      dn_norm_g)
    y = _outproj(x2, out_a, out_b, out_c, w_out[0].astype(BF16), final_g[None, :])
    return y[None]
```

```python
import functools

import jax
import jax.numpy as jnp
from jax import lax
from jax.experimental import pallas as pl
from jax.experimental.pallas import tpu as pltpu

F32 = jnp.float32
BF16 = jnp.bfloat16

D_MODEL = 2048
GMLP_CHUNK = 128
GMLP_GROUPS = 4
GMLP_WIDTH = 512
DN_HEADS = 8
DN_HEAD_DIM = 128
DN_WIDTH = 1024
DN_CONV = 4
XA_HEADS = 4
XA_HEAD_DIM = 128
XA_WIDTH = 512
EPS = 1e-6

LANES = 128
SUBLANES = 8
DN_BLOCK = 128
INV_BASE = 16

MAIN_WIDTH = 3 * GMLP_WIDTH + 4 * DN_WIDTH + 2 * XA_WIDTH
PROJ_TN = 512
PROJ_TM = 2048
NORM_TM = 512
PROJ_DIRECT_TILES = (3 * GMLP_WIDTH + 4 * DN_WIDTH) // PROJ_TN
COL_Q, COL_K, COL_DV, COL_DZ = 0, 2, 4, 6
COL_U, COL_V, COL_GZ = 8, 9, 10
COL_CQ, COL_CZ = 11, 12

VMEM_LIMIT = 48 * 1024 * 1024


def _dot(a, b):
    return jnp.dot(a, b, preferred_element_type=F32)


def _dot_nt(a, b):
    return lax.dot_general(a, b, (((1,), (1,)), ((), ())), preferred_element_type=F32)


def _silu(x):
    return x * jax.nn.sigmoid(x)


def _norm_kernel(x_ref, g_ref, wab_ref, h_ref, ab_ref, *, row_chunk):
    tm = x_ref.shape[0]

    def body(c, carry):
        r0 = pl.multiple_of(c * row_chunk, row_chunk)
        xf = x_ref[pl.ds(r0, row_chunk), :]
        ms = jnp.mean(xf * xf, axis=-1, keepdims=True)
        h_ref[pl.ds(r0, row_chunk), :] = (xf * lax.rsqrt(ms + EPS) * g_ref[...]).astype(BF16)
        return carry

    lax.fori_loop(0, tm // row_chunk, body, 0)
    ab_ref[...] = _dot_nt(h_ref[...], wab_ref[...].astype(BF16))


def _norm(x2, ln_g, w_ab):
    S = x2.shape[0]
    tm = min(NORM_TM, S)
    return pl.pallas_call(
        functools.partial(_norm_kernel, row_chunk=128),
        out_shape=(jax.ShapeDtypeStruct((S, D_MODEL), BF16),
                   jax.ShapeDtypeStruct((S, LANES), F32)),
        grid=(S // tm,),
        in_specs=[pl.BlockSpec((tm, D_MODEL), lambda m: (m, 0)),
                  pl.BlockSpec((1, D_MODEL), lambda m: (0, 0)),
                  pl.BlockSpec((LANES, D_MODEL), lambda m: (0, 0))],
        out_specs=(pl.BlockSpec((tm, D_MODEL), lambda m: (m, 0)),
                   pl.BlockSpec((tm, LANES), lambda m: (m, 0))),
        compiler_params=pltpu.CompilerParams(dimension_semantics=("arbitrary",),
                                             vmem_limit_bytes=VMEM_LIMIT),
        name="norm",
    )(x2, ln_g, w_ab)


def _proj_kernel(h_ref, wf_ref, wc_ref, p_ref):
    n = pl.program_id(1)

    @pl.when(n < PROJ_DIRECT_TILES)
    def _():
        p_ref[...] = _dot_nt(h_ref[...], wf_ref[...].astype(BF16)).astype(BF16)

    @pl.when(n >= PROJ_DIRECT_TILES)
    def _():
        p_ref[...] = _dot_nt(h_ref[...], wc_ref[...].astype(BF16)).astype(BF16)


def _proj_direct_block(n):
    dn_tiles = 4 * DN_WIDTH // PROJ_TN
    gm_tiles = 3 * GMLP_WIDTH // PROJ_TN
    last = PROJ_DIRECT_TILES - 1 - dn_tiles
    return jnp.where(n < dn_tiles, n + gm_tiles, jnp.minimum(n - dn_tiles, last))


def _proj(h, w_t, w_c):
    S = h.shape[0]
    tm = min(PROJ_TM, S)
    grid = (S // tm, MAIN_WIDTH // PROJ_TN)
    return pl.pallas_call(
        _proj_kernel,
        out_shape=jax.ShapeDtypeStruct((S, MAIN_WIDTH), BF16),
        grid=grid,
        in_specs=[
            pl.BlockSpec((tm, D_MODEL), lambda m, n: (m, 0)),
            pl.BlockSpec((PROJ_TN, D_MODEL), lambda m, n: (_proj_direct_block(n), 0)),
            pl.BlockSpec((PROJ_TN, D_MODEL), lambda m, n: (jnp.maximum(n - PROJ_DIRECT_TILES, 0), 0)),
        ],
        out_specs=pl.BlockSpec((tm, PROJ_TN), lambda m, n: (m, n)),
        compiler_params=pltpu.CompilerParams(
            dimension_semantics=("arbitrary", "arbitrary"),
            vmem_limit_bytes=VMEM_LIMIT),
        name="proj",
    )(h, w_t, w_c)


def _gmlp_kernel(u_ref, v_ref, z_ref, lng_ref, lnb_ref, ws_ref, bst_ref, o_ref):
    tm = u_ref.shape[0]
    T = GMLP_CHUNK
    v = jax.nn.gelu(v_ref[...].astype(F32))
    mu = jnp.mean(v, axis=-1, keepdims=True)
    vc = v - mu
    var = jnp.mean(vc * vc, axis=-1, keepdims=True)
    vn = (vc * lax.rsqrt(var + EPS) * lng_ref[...] + lnb_ref[...]).astype(BF16)
    row = lax.broadcasted_iota(jnp.int32, (T, T), 0)
    col = lax.broadcasted_iota(jnp.int32, (T, T), 1)
    causal = row >= col
    for g in range(GMLP_GROUPS):
        w = jnp.where(causal, ws_ref[g], 0.0).astype(BF16)
        bias = bst_ref[:, g:g + 1]
        cs = slice(g * LANES, (g + 1) * LANES)
        for c in range(tm // T):
            rs = slice(c * T, (c + 1) * T)
            s = _dot(w, vn[rs, cs]) + bias
            u = jax.nn.gelu(u_ref[rs, cs].astype(F32))
            o_ref[rs, cs] = (u * s * _silu(z_ref[rs, cs].astype(F32))).astype(BF16)


def _gmlp(p, ln_g, ln_b, ws, bs_t):
    S = p.shape[0]
    tm = min(512, S)
    blk = lambda j: pl.BlockSpec((tm, GMLP_WIDTH), lambda m, j=j: (m, j))
    full = lambda shp: pl.BlockSpec(shp, lambda m: (0,) * len(shp))
    return pl.pallas_call(
        _gmlp_kernel,
        out_shape=jax.ShapeDtypeStruct((S, GMLP_WIDTH), BF16),
        grid=(S // tm,),
        in_specs=[blk(COL_U), blk(COL_V), blk(COL_GZ),
                  full((1, GMLP_WIDTH)), full((1, GMLP_WIDTH)),
                  full((GMLP_GROUPS, GMLP_CHUNK, GMLP_CHUNK)), full((GMLP_CHUNK, GMLP_GROUPS))],
        out_specs=pl.BlockSpec((tm, GMLP_WIDTH), lambda m: (m, 0)),
        compiler_params=pltpu.CompilerParams(dimension_semantics=("arbitrary",),
                                             vmem_limit_bytes=VMEM_LIMIT),
        name="gmlp",
    )(p, p, p, ln_g, ln_b, ws, bs_t)


def _memkv_kernel(mem_ref, g_ref, w_ref, kv_ref):
    mf = mem_ref[...]
    ms = jnp.mean(mf * mf, axis=-1, keepdims=True)
    m = (mf * lax.rsqrt(ms + EPS) * g_ref[...]).astype(BF16)
    kv_ref[...] = _dot(m, w_ref[...]).astype(BF16)


def _memkv(mem2, g, w):
    M = mem2.shape[0]
    return pl.pallas_call(
        _memkv_kernel,
        out_shape=jax.ShapeDtypeStruct((M, 2 * XA_WIDTH), BF16),
        compiler_params=pltpu.CompilerParams(vmem_limit_bytes=VMEM_LIMIT),
        name="memkv",
    )(mem2, g, w)


def _xattn_kernel(q_ref, z_ref, kv_ref, o_ref):
    scale = XA_HEAD_DIM ** -0.5
    for h in range(XA_HEADS):
        cs = slice(h * LANES, (h + 1) * LANES)
        mk = kv_ref[:, h * LANES:(h + 1) * LANES]
        mv = kv_ref[:, XA_WIDTH + h * LANES:XA_WIDTH + (h + 1) * LANES]
        s = _dot_nt(q_ref[:, cs], mk) * scale
        mx = jnp.max(s, axis=-1, keepdims=True)
        e = jnp.exp(s - mx)
        l = jnp.sum(e, axis=-1, keepdims=True)
        o = _dot(e.astype(BF16), mv) / l
        o_ref[:, cs] = (o * _silu(z_ref[:, cs].astype(F32))).astype(BF16)


def _xattn(p, kv):
    S = p.shape[0]
    M = kv.shape[0]
    tm = min(512, S)
    return pl.pallas_call(
        _xattn_kernel,
        out_shape=jax.ShapeDtypeStruct((S, XA_WIDTH), BF16),
        grid=(S // tm,),
        in_specs=[pl.BlockSpec((tm, XA_WIDTH), lambda m: (m, COL_CQ)),
                  pl.BlockSpec((tm, XA_WIDTH), lambda m: (m, COL_CZ)),
                  pl.BlockSpec((M, 2 * XA_WIDTH), lambda m: (0, 0))],
        out_specs=pl.BlockSpec((tm, XA_WIDTH), lambda m: (m, 0)),
        compiler_params=pltpu.CompilerParams(dimension_semantics=("arbitrary",),
                                             vmem_limit_bytes=VMEM_LIMIT),
        name="xattn",
    )(p, p, kv)


def _unit_lower_inverse(a_list, masks, eye):
    base_mask, merge_masks = masks
    n_mat = range(len(a_list))
    x = [jnp.where(base_mask, -a, 0.0) for a in a_list]
    xb = [v.astype(BF16) for v in x]
    t = [eye + v for v in x]
    n = 2
    while n < INV_BASE:
        x = [_dot(xb[j], xb[j]) for j in n_mat]
        xb = [v.astype(BF16) for v in x]
        tx = [_dot(t[j].astype(BF16), xb[j]) for j in n_mat]
        t = [t[j] + tx[j] for j in n_mat]
        n *= 2
    for m in merge_masks:
        al = [jnp.where(m, a, 0.0).astype(BF16) for a in a_list]
        tb = [v.astype(BF16) for v in t]
        u = [_dot(tb[j], al[j]).astype(BF16) for j in n_mat]
        w = [_dot(u[j], tb[j]) for j in n_mat]
        t = [t[j] - w[j] for j in n_mat]
    return t


def _dn_masks():
    C = DN_BLOCK
    row = lax.broadcasted_iota(jnp.int32, (C, C), 0)
    col = lax.broadcasted_iota(jnp.int32, (C, C), 1)
    return row, col


def _dn_stage(q_ref, k_ref, v_ref, xe_ref):
    C = DN_BLOCK
    for j, ref in enumerate((q_ref, k_ref, v_ref)):
        for h in range(DN_HEADS):
            t = j * DN_HEADS + h
            xe_ref[t, 0:SUBLANES, :] = xe_ref[t, C:C + SUBLANES, :]
            xe_ref[t, SUBLANES:SUBLANES + C, :] = ref[:, h * LANES:(h + 1) * LANES].astype(F32)


def _dn_prep(ab_ref, cw_ref, alog_ref, dtb_ref, xe_ref, slot):
    C = DN_BLOCK
    qb_s, qgb_s, kbb_s, ktb_s, kdtb_s, rhs_s, decay_s, g8_s = slot
    row, col = _dn_masks()
    tri = row >= col

    ab = ab_ref[...]
    lane = lax.broadcasted_iota(jnp.int32, (C, LANES), 1)
    xa = ab + dtb_ref[...]
    softplus = jnp.maximum(xa, 0.0) + jnp.log1p(jnp.exp(-jnp.abs(xa)))
    g = jnp.where(lane < DN_HEADS, -jnp.exp(alog_ref[...]) * softplus, 0.0)
    beta_t = jax.nn.sigmoid(ab)

    ones_l = tri.astype(BF16)
    g1 = g.astype(BF16)
    r1 = g - g1.astype(F32)
    g2 = r1.astype(BF16)
    g3 = (r1 - g2.astype(F32)).astype(BF16)
    gc = _dot(ones_l, g1) + _dot(ones_l, g2) + _dot(ones_l, g3)
    egc_t = jnp.exp(gc)
    gct8 = gc.T[0:SUBLANES, :]
    edl8 = jnp.exp(gct8[:, C - 1:C] - gct8)
    g8_s[...] = gct8

    def conv_silu(c0, h):
        t = (c0 // DN_WIDTH) * DN_HEADS + h
        ts = slice(c0 + h * LANES, c0 + (h + 1) * LANES)
        acc = xe_ref[t, SUBLANES:SUBLANES + C, :] * cw_ref[DN_CONV - 1:DN_CONV, ts]
        for s in range(1, DN_CONV):
            acc = acc + xe_ref[t, SUBLANES - s:SUBLANES - s + C, :] * cw_ref[DN_CONV - 1 - s:DN_CONV - s, ts]
        return _silu(acc)

    for h in range(DN_HEADS):
        q = conv_silu(0, h)
        k = conv_silu(DN_WIDTH, h)
        v = conv_silu(2 * DN_WIDTH, h)
        q = q * (lax.rsqrt(jnp.sum(q * q, axis=-1, keepdims=True) + EPS) * (DN_HEAD_DIM ** -0.5))
        k = k * lax.rsqrt(jnp.sum(k * k, axis=-1, keepdims=True) + EPS)
        beta = beta_t[:, DN_HEADS + h:DN_HEADS + h + 1]
        egc = egc_t[:, h:h + 1]
        kb = k * beta
        kt = k.T
        qb_s[h] = q.astype(BF16)
        qgb_s[h] = (q * egc).astype(BF16)
        kbb_s[h] = kb.astype(BF16)
        ktb_s[h] = kt.astype(BF16)
        kdtb_s[h] = (kt * edl8[h:h + 1, :]).astype(BF16)
        rhs_s[h] = jnp.concatenate([v * beta, kb * egc], axis=1).astype(BF16)
        decay_s[h] = jnp.exp(jnp.where(tri, gc[:, h:h + 1] - gct8[h:h + 1, :], -jnp.inf))


def _dn_chain(z_ref, ng_ref, o_ref, state_ref, slot):
    C = DN_BLOCK
    heads = range(DN_HEADS)
    qb_s, qgb_s, kbb_s, ktb_s, kdtb_s, rhs_s, decay_s, g8_s = slot
    row, col = _dn_masks()
    strict = row > col
    eye = (row == col).astype(F32)
    same = lambda b: (row // b) == (col // b)
    merge_masks = []
    b = INV_BASE
    while b < C:
        merge_masks.append(jnp.logical_and(same(2 * b), jnp.logical_not(same(b))))
        b *= 2
    masks = (same(INV_BASE), merge_masks)

    kk = [_dot(kbb_s[h], ktb_s[h]) for h in heads]
    qk = [_dot(qb_s[h], ktb_s[h]) for h in heads]
    a = [jnp.where(strict, kk[h] * decay_s[h], 0.0) for h in heads]
    aib = [(qk[h] * decay_s[h]).astype(BF16) for h in heads]
    t = _unit_lower_inverse(a, masks, eye)

    tv = [_dot(t[h].astype(BF16), rhs_s[h]) for h in heads]
    stb = [state_ref[h].astype(BF16) for h in heads]
    ks = [_dot(tv[h][:, LANES:].astype(BF16), stb[h]) for h in heads]
    o_inter = [_dot(qgb_s[h], stb[h]) for h in heads]
    vnb = [(tv[h][:, :LANES] - ks[h]).astype(BF16) for h in heads]
    o_intra = [_dot(aib[h], vnb[h]) for h in heads]
    upd = [_dot(kdtb_s[h], vnb[h]) for h in heads]
    for h in heads:
        cs = slice(h * LANES, (h + 1) * LANES)
        egl = jnp.exp(g8_s[h:h + 1, C - 1:C])
        state_ref[h] = state_ref[h] * egl + upd[h]
        o = o_inter[h] + o_intra[h]
        on = o * lax.rsqrt(jnp.mean(o * o, axis=-1, keepdims=True) + EPS) * ng_ref[...]
        o_ref[:, cs] = (on * _silu(z_ref[:, cs].astype(F32))).astype(BF16)


def _dn_kernel(q_ref, k_ref, v_ref, z_ref, ab_ref, cw_ref, alog_ref, dtb_ref, ng_ref,
               o_ref, state_ref, xe_ref, *slots):
    n_slot_refs = len(slots) // 2
    slot_a, slot_b = slots[:n_slot_refs], slots[n_slot_refs:]
    j = pl.program_id(0)

    @pl.when(j == 0)
    def _():
        state_ref[...] = jnp.zeros_like(state_ref)
        xe_ref[...] = jnp.zeros_like(xe_ref)
        for ref in slot_a:
            ref[...] = jnp.zeros_like(ref)

    def step(prep_slot, chain_slot):
        _dn_prep(ab_ref, cw_ref, alog_ref, dtb_ref, xe_ref, prep_slot)
        _dn_chain(z_ref, ng_ref, o_ref, state_ref, chain_slot)
        _dn_stage(q_ref, k_ref, v_ref, xe_ref)

    @pl.when(j % 2 == 0)
    def _():
        step(slot_b, slot_a)

    @pl.when(j % 2 == 1)
    def _():
        step(slot_a, slot_b)


def _deltanet(p, ab, conv_w, alog, dtb, ng):
    S = p.shape[0]
    C = DN_BLOCK
    nb = S // C
    clip = lambda b: jnp.clip(b, 0, nb - 1)
    blk = lambda c: pl.BlockSpec((C, DN_WIDTH), lambda j, c=c: (clip(j), c))
    full = lambda shp: pl.BlockSpec(shp, lambda j: (0,) * len(shp))
    hb = lambda w, dt: pltpu.VMEM((DN_HEADS, DN_HEAD_DIM, w), dt)
    slot = [hb(DN_HEAD_DIM, BF16)] * 5 + [hb(2 * DN_HEAD_DIM, BF16), hb(DN_HEAD_DIM, F32),
                                           pltpu.VMEM((SUBLANES, C), F32)]
    return pl.pallas_call(
        _dn_kernel,
        out_shape=jax.ShapeDtypeStruct((S, DN_WIDTH), BF16),
        grid=(nb + 2,),
        in_specs=[blk(COL_Q // 2), blk(COL_K // 2), blk(COL_DV // 2),
                  pl.BlockSpec((C, DN_WIDTH), lambda j: (clip(j - 2), COL_DZ // 2)),
                  pl.BlockSpec((C, LANES), lambda j: (clip(j - 1), 0)),
                  full((DN_CONV, 3 * DN_WIDTH)), full((1, LANES)), full((1, LANES)),
                  full((1, DN_HEAD_DIM))],
        out_specs=pl.BlockSpec((C, DN_WIDTH), lambda j: (clip(j - 2), 0)),
        scratch_shapes=[pltpu.VMEM((DN_HEADS, DN_HEAD_DIM, DN_HEAD_DIM), F32),
                        pltpu.VMEM((3 * DN_HEADS, SUBLANES + C, LANES), F32)] + slot + slot,
        compiler_params=pltpu.CompilerParams(dimension_semantics=("arbitrary",),
                                             vmem_limit_bytes=VMEM_LIMIT),
        name="deltanet",
    )(p, p, p, p, ab, conv_w, alog, dtb, ng)


def _out_kernel(x_ref, a_ref, b_ref, c_ref, w_ref, fg_ref, y_ref):
    acc = _dot(a_ref[...], w_ref[0:GMLP_WIDTH, :])
    acc = acc + _dot(b_ref[...], w_ref[GMLP_WIDTH:GMLP_WIDTH + DN_WIDTH, :])
    acc = acc + _dot(c_ref[...], w_ref[GMLP_WIDTH + DN_WIDTH:, :])
    r = x_ref[...] + acc
    ms = jnp.mean(r * r, axis=-1, keepdims=True)
    y_ref[...] = r * lax.rsqrt(ms + EPS) * fg_ref[...]


def _outproj(x2, oa, ob, oc, w_out, fg):
    S = x2.shape[0]
    tm = min(512, S)
    row = lambda w: pl.BlockSpec((tm, w), lambda m: (m, 0))
    return pl.pallas_call(
        _out_kernel,
        out_shape=jax.ShapeDtypeStruct((S, D_MODEL), F32),
        grid=(S // tm,),
        in_specs=[row(D_MODEL), row(GMLP_WIDTH), row(DN_WIDTH), row(XA_WIDTH),
                  pl.BlockSpec((D_MODEL, D_MODEL), lambda m: (0, 0)),
                  pl.BlockSpec((1, D_MODEL), lambda m: (0, 0))],
        out_specs=row(D_MODEL),
        compiler_params=pltpu.CompilerParams(dimension_semantics=("arbitrary",),
                                             vmem_limit_bytes=VMEM_LIMIT),
        name="outproj",
    )(x2, oa, ob, oc, w_out, fg)


def _pad_lanes(v):
    return jnp.pad(v.astype(F32), (0, LANES - v.shape[0]))[None, :]


def kernel(x, mem, ln_g, w_in, gmlp_ln_g, gmlp_ln_b, gmlp_ws, gmlp_bs, conv_w, dn_a_log,
           dn_dt_bias, dn_norm_g, mem_norm_g, w_mem_kv, w_out, final_g):
    B, S, _ = x.shape
    assert B == 1 and ln_g.shape[0] == 1 and S % DN_BLOCK == 0
    x2 = x[0]
    w_t = jnp.swapaxes(w_in, 1, 2)[0]
    ab0 = MAIN_WIDTH - 2 * XA_WIDTH
    w_c = w_t[ab0 + 2 * DN_HEADS:]
    w_ab = jnp.pad(w_t[ab0:ab0 + 2 * DN_HEADS], ((0, LANES - 2 * DN_HEADS), (0, 0)))

    h, ab = _norm(x2, ln_g, w_ab)
    p = _proj(h, w_t, w_c)
    out_a = _gmlp(p, gmlp_ln_g, gmlp_ln_b, gmlp_ws[0], gmlp_bs[0].T)
    kv = _memkv(mem[0], mem_norm_g, w_mem_kv[0].astype(BF16))
    out_c = _xattn(p, kv)
    out_b = _deltanet(p, ab, conv_w[0], _pad_lanes(dn_a_log[0]), _pad_lanes(dn_dt_bias[0]),
                      dn_norm_g)
    y = _outproj(x2, out_a, out_b, out_c, w_out[0].astype(BF16), final_g[None, :])
    return y[None]
```

```python
import functools

import jax
import jax.numpy as jnp
from jax import lax
from jax.experimental import pallas as pl
from jax.experimental.pallas import tpu as pltpu

F32 = jnp.float32
BF16 = jnp.bfloat16

D_MODEL = 2048
GMLP_CHUNK = 128
GMLP_GROUPS = 4
GMLP_WIDTH = 512
DN_HEADS = 8
DN_HEAD_DIM = 128
DN_WIDTH = 1024
DN_CONV = 4
XA_HEADS = 4
XA_HEAD_DIM = 128
XA_WIDTH = 512
EPS = 1e-6

LANES = 128
SUBLANES = 8
DN_BLOCK = 128
INV_BASE = 16
DN_PAIR = 2

PROJ_TN = 512
PROJ_TM = 2048
NORM_TM = 512
EPI_ROWS = 256

W_BLOCK_QKV0 = 3
QKV_TILES = 3 * DN_WIDTH // PROJ_TN
W_BLOCK_DZ0 = 9
GATE_COL0 = 3 * GMLP_WIDTH + 4 * DN_WIDTH

REST_WIDTH = DN_WIDTH + 3 * GMLP_WIDTH + 2 * XA_WIDTH
REST_TILES = REST_WIDTH // PROJ_TN
REST_DIRECT_TILES = (DN_WIDTH + 3 * GMLP_WIDTH) // PROJ_TN
COL_DZ = 0
COL_U, COL_V, COL_GZ = 2, 3, 4
COL_CQ, COL_CZ = 5, 6

VMEM_LIMIT = 48 * 1024 * 1024


def _dot(a, b):
    return jnp.dot(a, b, preferred_element_type=F32)


def _dot_nt(a, b):
    return lax.dot_general(a, b, (((1,), (1,)), ((), ())), preferred_element_type=F32)


def _silu(x):
    return x * jax.nn.sigmoid(x)


def _norm_kernel(x_ref, g_ref, wab_ref, h_ref, ab_ref, *, row_chunk):
    tm = x_ref.shape[0]

    def body(c, carry):
        r0 = pl.multiple_of(c * row_chunk, row_chunk)
        xf = x_ref[pl.ds(r0, row_chunk), :]
        ms = jnp.mean(xf * xf, axis=-1, keepdims=True)
        h_ref[pl.ds(r0, row_chunk), :] = (xf * lax.rsqrt(ms + EPS) * g_ref[...]).astype(BF16)
        return carry

    lax.fori_loop(0, tm // row_chunk, body, 0)
    ab_ref[...] = _dot_nt(h_ref[...], wab_ref[...].astype(BF16))


def _norm(x2, ln_g, w_ab):
    S = x2.shape[0]
    tm = min(NORM_TM, S)
    return pl.pallas_call(
        functools.partial(_norm_kernel, row_chunk=128),
        out_shape=(jax.ShapeDtypeStruct((S, D_MODEL), BF16),
                   jax.ShapeDtypeStruct((S, LANES), F32)),
        grid=(S // tm,),
        in_specs=[pl.BlockSpec((tm, D_MODEL), lambda m: (m, 0)),
                  pl.BlockSpec((1, D_MODEL), lambda m: (0, 0)),
                  pl.BlockSpec((LANES, D_MODEL), lambda m: (0, 0))],
        out_specs=(pl.BlockSpec((tm, D_MODEL), lambda m: (m, 0)),
                   pl.BlockSpec((tm, LANES), lambda m: (m, 0))),
        compiler_params=pltpu.CompilerParams(dimension_semantics=("arbitrary",),
                                             vmem_limit_bytes=VMEM_LIMIT),
        name="norm",
    )(x2, ln_g, w_ab)


def _qkv_epilogue(tile, slab, cw_ref, tail_ref, o_ref):
    tm = o_ref.shape[0]
    n = tile % QKV_TILES
    first = tile < QKV_TILES
    is_v = n >= 2 * (QKV_TILES // 3)
    for c in range(PROJ_TN // LANES):
        cs = slice(c * LANES, (c + 1) * LANES)
        slab[c, 0:SUBLANES, :] = jnp.where(first, 0.0, tail_ref[n, c])
        tail_ref[n, c] = slab[c, tm:tm + SUBLANES, :]
        for r0 in range(0, tm, EPI_ROWS):
            acc = slab[c, SUBLANES + r0:SUBLANES + r0 + EPI_ROWS, :] * cw_ref[DN_CONV - 1:DN_CONV, cs]
            for s in range(1, DN_CONV):
                acc = acc + (slab[c, SUBLANES + r0 - s:SUBLANES + r0 - s + EPI_ROWS, :]
                             * cw_ref[DN_CONV - 1 - s:DN_CONV - s, cs])
            y = _silu(acc)
            inv = lax.rsqrt(jnp.sum(y * y, axis=-1, keepdims=True) + EPS)
            o_ref[r0:r0 + EPI_ROWS, cs] = (y * jnp.where(is_v, 1.0, inv)).astype(BF16)


def _qkv_kernel(h_ref, w_ref, cw_ref, o_ref, slab_a, slab_b, tail_ref):
    t = pl.program_id(0)
    tm = h_ref.shape[0]

    @pl.when(t == 0)
    def _():
        slab_b[...] = jnp.zeros_like(slab_b)
        tail_ref[...] = jnp.zeros_like(tail_ref)

    def step(dot_slab, epi_slab):
        raw = _dot_nt(h_ref[...], w_ref[...].astype(BF16))
        for c in range(PROJ_TN // LANES):
            dot_slab[c, SUBLANES:SUBLANES + tm, :] = raw[:, c * LANES:(c + 1) * LANES]
        _qkv_epilogue(jnp.maximum(t - 1, 0), epi_slab, cw_ref, tail_ref, o_ref)

    @pl.when(t % 2 == 0)
    def _():
        step(slab_a, slab_b)

    @pl.when(t % 2 == 1)
    def _():
        step(slab_b, slab_a)


def _qkv(h, w_t, conv_w):
    S = h.shape[0]
    tm = min(PROJ_TM, S)
    n_tiles = (S // tm) * QKV_TILES
    cur = lambda t: jnp.minimum(t, n_tiles - 1)
    prev = lambda t: jnp.maximum(t - 1, 0)
    slab = pltpu.VMEM((PROJ_TN // LANES, SUBLANES + tm, LANES), F32)
    return pl.pallas_call(
        _qkv_kernel,
        out_shape=jax.ShapeDtypeStruct((S, 3 * DN_WIDTH), BF16),
        grid=(n_tiles + 1,),
        in_specs=[
            pl.BlockSpec((tm, D_MODEL), lambda t: (cur(t) // QKV_TILES, 0)),
            pl.BlockSpec((PROJ_TN, D_MODEL), lambda t: (W_BLOCK_QKV0 + cur(t) % QKV_TILES, 0)),
            pl.BlockSpec((DN_CONV, PROJ_TN), lambda t: (0, prev(t) % QKV_TILES)),
        ],
        out_specs=pl.BlockSpec((tm, PROJ_TN), lambda t: (prev(t) // QKV_TILES, prev(t) % QKV_TILES)),
        scratch_shapes=[slab, slab,
                        pltpu.VMEM((QKV_TILES, PROJ_TN // LANES, SUBLANES, LANES), F32)],
        compiler_params=pltpu.CompilerParams(dimension_semantics=("arbitrary",),
                                             vmem_limit_bytes=VMEM_LIMIT),
        name="qkv",
    )(h, w_t, conv_w)


def _proj_kernel(h_ref, wf_ref, wc_ref, p_ref):
    n = pl.program_id(1)

    @pl.when(n < REST_DIRECT_TILES)
    def _():
        p_ref[...] = _dot_nt(h_ref[...], wf_ref[...].astype(BF16)).astype(BF16)

    @pl.when(n >= REST_DIRECT_TILES)
    def _():
        p_ref[...] = _dot_nt(h_ref[...], wc_ref[...].astype(BF16)).astype(BF16)


def _proj_direct_block(n):
    dz_tiles = DN_WIDTH // PROJ_TN
    last = REST_DIRECT_TILES - 1 - dz_tiles
    return jnp.where(n < dz_tiles, n + W_BLOCK_DZ0, jnp.minimum(n - dz_tiles, last))


def _proj(h, w_t, w_c):
    S = h.shape[0]
    tm = min(PROJ_TM, S)
    return pl.pallas_call(
        _proj_kernel,
        out_shape=jax.ShapeDtypeStruct((S, REST_WIDTH), BF16),
        grid=(S // tm, REST_TILES),
        in_specs=[
            pl.BlockSpec((tm, D_MODEL), lambda m, n: (m, 0)),
            pl.BlockSpec((PROJ_TN, D_MODEL), lambda m, n: (_proj_direct_block(n), 0)),
            pl.BlockSpec((PROJ_TN, D_MODEL), lambda m, n: (jnp.maximum(n - REST_DIRECT_TILES, 0), 0)),
        ],
        out_specs=pl.BlockSpec((tm, PROJ_TN), lambda m, n: (m, n)),
        compiler_params=pltpu.CompilerParams(
            dimension_semantics=("arbitrary", "arbitrary"),
            vmem_limit_bytes=VMEM_LIMIT),
        name="proj",
    )(h, w_t, w_c)


def _gmlp_kernel(u_ref, v_ref, z_ref, lng_ref, lnb_ref, ws_ref, bst_ref, o_ref):
    tm = u_ref.shape[0]
    T = GMLP_CHUNK
    v = jax.nn.gelu(v_ref[...].astype(F32))
    mu = jnp.mean(v, axis=-1, keepdims=True)
    vc = v - mu
    var = jnp.mean(vc * vc, axis=-1, keepdims=True)
    vn = (vc * lax.rsqrt(var + EPS) * lng_ref[...] + lnb_ref[...]).astype(BF16)
    row = lax.broadcasted_iota(jnp.int32, (T, T), 0)
    col = lax.broadcasted_iota(jnp.int32, (T, T), 1)
    causal = row >= col
    for g in range(GMLP_GROUPS):
        w = jnp.where(causal, ws_ref[g], 0.0).astype(BF16)
        bias = bst_ref[:, g:g + 1]
        cs = slice(g * LANES, (g + 1) * LANES)
        for c in range(tm // T):
            rs = slice(c * T, (c + 1) * T)
            s = _dot(w, vn[rs, cs]) + bias
            u = jax.nn.gelu(u_ref[rs, cs].astype(F32))
            o_ref[rs, cs] = (u * s * _silu(z_ref[rs, cs].astype(F32))).astype(BF16)


def _gmlp(p, ln_g, ln_b, ws, bs_t):
    S = p.shape[0]
    tm = min(512, S)
    blk = lambda j: pl.BlockSpec((tm, GMLP_WIDTH), lambda m, j=j: (m, j))
    full = lambda shp: pl.BlockSpec(shp, lambda m: (0,) * len(shp))
    return pl.pallas_call(
        _gmlp_kernel,
        out_shape=jax.ShapeDtypeStruct((S, GMLP_WIDTH), BF16),
        grid=(S // tm,),
        in_specs=[blk(COL_U), blk(COL_V), blk(COL_GZ),
                  full((1, GMLP_WIDTH)), full((1, GMLP_WIDTH)),
                  full((GMLP_GROUPS, GMLP_CHUNK, GMLP_CHUNK)), full((GMLP_CHUNK, GMLP_GROUPS))],
        out_specs=pl.BlockSpec((tm, GMLP_WIDTH), lambda m: (m, 0)),
        compiler_params=pltpu.CompilerParams(dimension_semantics=("arbitrary",),
                                             vmem_limit_bytes=VMEM_LIMIT),
        name="gmlp",
    )(p, p, p, ln_g, ln_b, ws, bs_t)


def _memkv_kernel(mem_ref, g_ref, w_ref, kv_ref):
    mf = mem_ref[...]
    ms = jnp.mean(mf * mf, axis=-1, keepdims=True)
    m = (mf * lax.rsqrt(ms + EPS) * g_ref[...]).astype(BF16)
    kv_ref[...] = _dot(m, w_ref[...]).astype(BF16)


def _memkv(mem2, g, w):
    M = mem2.shape[0]
    return pl.pallas_call(
        _memkv_kernel,
        out_shape=jax.ShapeDtypeStruct((M, 2 * XA_WIDTH), BF16),
        compiler_params=pltpu.CompilerParams(vmem_limit_bytes=VMEM_LIMIT),
        name="memkv",
    )(mem2, g, w)


def _xattn_kernel(q_ref, z_ref, kv_ref, o_ref):
    scale = XA_HEAD_DIM ** -0.5
    for h in range(XA_HEADS):
        cs = slice(h * LANES, (h + 1) * LANES)
        mk = kv_ref[:, h * LANES:(h + 1) * LANES]
        mv = kv_ref[:, XA_WIDTH + h * LANES:XA_WIDTH + (h + 1) * LANES]
        s = _dot_nt(q_ref[:, cs], mk) * scale
        mx = jnp.max(s, axis=-1, keepdims=True)
        e = jnp.exp(s - mx)
        l = jnp.sum(e, axis=-1, keepdims=True)
        o = _dot(e.astype(BF16), mv) / l
        o_ref[:, cs] = (o * _silu(z_ref[:, cs].astype(F32))).astype(BF16)


def _xattn(p, kv):
    S = p.shape[0]
    M = kv.shape[0]
    tm = min(512, S)
    return pl.pallas_call(
        _xattn_kernel,
        out_shape=jax.ShapeDtypeStruct((S, XA_WIDTH), BF16),
        grid=(S // tm,),
        in_specs=[pl.BlockSpec((tm, XA_WIDTH), lambda m: (m, COL_CQ)),
                  pl.BlockSpec((tm, XA_WIDTH), lambda m: (m, COL_CZ)),
                  pl.BlockSpec((M, 2 * XA_WIDTH), lambda m: (0, 0))],
        out_specs=pl.BlockSpec((tm, XA_WIDTH), lambda m: (m, 0)),
        compiler_params=pltpu.CompilerParams(dimension_semantics=("arbitrary",),
                                             vmem_limit_bytes=VMEM_LIMIT),
        name="xattn",
    )(p, p, kv)


def _unit_lower_inverse(a_list, masks, eye):
    base_mask, merge_masks = masks
    n_mat = range(len(a_list))
    x = [jnp.where(base_mask, -a, 0.0) for a in a_list]
    xb = [v.astype(BF16) for v in x]
    t = [eye + v for v in x]
    n = 2
    while n < INV_BASE:
        x = [_dot(xb[j], xb[j]) for j in n_mat]
        xb = [v.astype(BF16) for v in x]
        tx = [_dot(t[j].astype(BF16), xb[j]) for j in n_mat]
        t = [t[j] + tx[j] for j in n_mat]
        n *= 2
    for m in merge_masks:
        al = [jnp.where(m, a, 0.0).astype(BF16) for a in a_list]
        tb = [v.astype(BF16) for v in t]
        u = [_dot(tb[j], al[j]).astype(BF16) for j in n_mat]
        w = [_dot(u[j], tb[j]) for j in n_mat]
        t = [t[j] - w[j] for j in n_mat]
    return t


def _dn_masks():
    C = DN_BLOCK
    row = lax.broadcasted_iota(jnp.int32, (C, C), 0)
    col = lax.broadcasted_iota(jnp.int32, (C, C), 1)
    return row, col


def _dn_prep(q_ref, k_ref, v_ref, ab_ref, alog_ref, dtb_ref, slot, blk):
    C = DN_BLOCK
    rs = slice(blk * C, (blk + 1) * C)
    qb_s, qgb_s, kbb_s, ktb_s, kdtb_s, rhs_s, decay_s, g8_s = slot
    row, col = _dn_masks()
    tri = row >= col

    ab = ab_ref[rs, :]
    lane = lax.broadcasted_iota(jnp.int32, (C, LANES), 1)
    xa = ab + dtb_ref[...]
    softplus = jnp.maximum(xa, 0.0) + jnp.log1p(jnp.exp(-jnp.abs(xa)))
    g = jnp.where(lane < DN_HEADS, -jnp.exp(alog_ref[...]) * softplus, 0.0)
    beta_t = jax.nn.sigmoid(ab)

    ones_l = tri.astype(BF16)
    g1 = g.astype(BF16)
    r1 = g - g1.astype(F32)
    g2 = r1.astype(BF16)
    g3 = (r1 - g2.astype(F32)).astype(BF16)
    gc = _dot(ones_l, g1) + _dot(ones_l, g2) + _dot(ones_l, g3)
    egc_t = jnp.exp(gc)
    gct8 = gc.T[0:SUBLANES, :]
    edl8 = jnp.exp(gct8[:, C - 1:C] - gct8)
    g8_s[blk] = gct8

    for h in range(DN_HEADS):
        cs = slice(h * LANES, (h + 1) * LANES)
        i = blk * DN_HEADS + h
        q = q_ref[rs, cs]
        k = k_ref[rs, cs].astype(F32)
        v = v_ref[rs, cs].astype(F32)
        beta = beta_t[:, DN_HEADS + h:DN_HEADS + h + 1]
        egc = egc_t[:, h:h + 1]
        kb = k * beta
        kt = k.T
        qb_s[i] = q
        qgb_s[i] = (q.astype(F32) * egc).astype(BF16)
        kbb_s[i] = kb.astype(BF16)
        ktb_s[i] = kt.astype(BF16)
        kdtb_s[i] = (kt * edl8[h:h + 1, :]).astype(BF16)
        rhs_s[i] = jnp.concatenate([v * beta, kb * egc], axis=1).astype(BF16)
        decay_s[i] = jnp.exp(jnp.where(tri, gc[:, h:h + 1] - gct8[h:h + 1, :], -jnp.inf))


def _dn_chain(z_ref, ng_ref, o_ref, state_ref, slot):
    C = DN_BLOCK
    heads = range(DN_HEADS)
    mats = range(DN_PAIR * DN_HEADS)
    qb_s, qgb_s, kbb_s, ktb_s, kdtb_s, rhs_s, decay_s, g8_s = slot
    row, col = _dn_masks()
    strict = row > col
    eye = (row == col).astype(F32)
    same = lambda b: (row // b) == (col // b)
    merge_masks = []
    b = INV_BASE
    while b < C:
        merge_masks.append(jnp.logical_and(same(2 * b), jnp.logical_not(same(b))))
        b *= 2
    masks = (same(INV_BASE), merge_masks)

    kk = [_dot(kbb_s[i], ktb_s[i]) for i in mats]
    qk = [_dot(qb_s[i], ktb_s[i]) for i in mats]
    a = [jnp.where(strict, kk[i] * decay_s[i], 0.0) for i in mats]
    aib = [(qk[i] * decay_s[i]).astype(BF16) for i in mats]
    t = _unit_lower_inverse(a, masks, eye)
    tv = [_dot(t[i].astype(BF16), rhs_s[i]) for i in mats]

    for blk in range(DN_PAIR):
        rs = slice(blk * C, (blk + 1) * C)
        m = [blk * DN_HEADS + h for h in heads]
        stb = [state_ref[h].astype(BF16) for h in heads]
        ks = [_dot(tv[m[h]][:, LANES:].astype(BF16), stb[h]) for h in heads]
        o_inter = [_dot(qgb_s[m[h]], stb[h]) for h in heads]
        vnb = [(tv[m[h]][:, :LANES] - ks[h]).astype(BF16) for h in heads]
        o_intra = [_dot(aib[m[h]], vnb[h]) for h in heads]
        upd = [_dot(kdtb_s[m[h]], vnb[h]) for h in heads]
        for h in heads:
            cs = slice(h * LANES, (h + 1) * LANES)
            egl = jnp.exp(g8_s[blk, h:h + 1, C - 1:C])
            state_ref[h] = state_ref[h] * egl + upd[h]
            o = o_inter[h] + o_intra[h]
            on = o * lax.rsqrt(jnp.mean(o * o, axis=-1, keepdims=True) + EPS * DN_HEAD_DIM) * ng_ref[...]
            o_ref[rs, cs] = (on * _silu(z_ref[rs, cs].astype(F32))).astype(BF16)


def _dn_kernel(q_ref, k_ref, v_ref, z_ref, ab_ref, alog_ref, dtb_ref, ng_ref,
               o_ref, state_ref, *slots):
    n_slot_refs = len(slots) // 2
    slot_a, slot_b = slots[:n_slot_refs], slots[n_slot_refs:]
    j = pl.program_id(0)

    @pl.when(j == 0)
    def _():
        state_ref[...] = jnp.zeros_like(state_ref)
        for ref in slot_b:
            ref[...] = jnp.zeros_like(ref)

    def step(prep_slot, chain_slot):
        for blk in range(DN_PAIR):
            _dn_prep(q_ref, k_ref, v_ref, ab_ref, alog_ref, dtb_ref, prep_slot, blk)
        _dn_chain(z_ref, ng_ref, o_ref, state_ref, chain_slot)

    @pl.when(j % 2 == 0)
    def _():
        step(slot_a, slot_b)

    @pl.when(j % 2 == 1)
    def _():
        step(slot_b, slot_a)


def _deltanet(qkv, p, ab, alog, dtb, ng):
    S = qkv.shape[0]
    R = DN_PAIR * DN_BLOCK
    nb = S // R
    cur = lambda j: jnp.minimum(j, nb - 1)
    prev = lambda j: jnp.maximum(j - 1, 0)
    blk = lambda c: pl.BlockSpec((R, DN_WIDTH), lambda j, c=c: (cur(j), c))
    full = lambda shp: pl.BlockSpec(shp, lambda j: (0,) * len(shp))
    hb = lambda w, dt: pltpu.VMEM((DN_PAIR * DN_HEADS, DN_HEAD_DIM, w), dt)
    slot = [hb(DN_HEAD_DIM, BF16)] * 5 + [hb(2 * DN_HEAD_DIM, BF16), hb(DN_HEAD_DIM, F32),
                                           pltpu.VMEM((DN_PAIR, SUBLANES, DN_BLOCK), F32)]
    return pl.pallas_call(
        _dn_kernel,
        out_shape=jax.ShapeDtypeStruct((S, DN_WIDTH), BF16),
        grid=(nb + 1,),
        in_specs=[blk(0), blk(1), blk(2),
                  pl.BlockSpec((R, DN_WIDTH), lambda j: (prev(j), COL_DZ)),
                  pl.BlockSpec((R, LANES), lambda j: (cur(j), 0)),
                  full((1, LANES)), full((1, LANES)), full((1, DN_HEAD_DIM))],
        out_specs=pl.BlockSpec((R, DN_WIDTH), lambda j: (prev(j), 0)),
        scratch_shapes=[pltpu.VMEM((DN_HEADS, DN_HEAD_DIM, DN_HEAD_DIM), F32)] + slot + slot,
        compiler_params=pltpu.CompilerParams(dimension_semantics=("arbitrary",),
                                             vmem_limit_bytes=VMEM_LIMIT),
        name="deltanet",
    )(qkv, qkv, qkv, p, ab, alog, dtb, ng)


def _out_kernel(x_ref, a_ref, b_ref, c_ref, w_ref, fg_ref, y_ref):
    acc = _dot(a_ref[...], w_ref[0:GMLP_WIDTH, :])
    acc = acc + _dot(b_ref[...], w_ref[GMLP_WIDTH:GMLP_WIDTH + DN_WIDTH, :])
    acc = acc + _dot(c_ref[...], w_ref[GMLP_WIDTH + DN_WIDTH:, :])
    r = x_ref[...] + acc
    ms = jnp.mean(r * r, axis=-1, keepdims=True)
    y_ref[...] = r * lax.rsqrt(ms + EPS) * fg_ref[...]


def _outproj(x2, oa, ob, oc, w_out, fg):
    S = x2.shape[0]
    tm = min(512, S)
    row = lambda w: pl.BlockSpec((tm, w), lambda m: (m, 0))
    return pl.pallas_call(
        _out_kernel,
        out_shape=jax.ShapeDtypeStruct((S, D_MODEL), F32),
        grid=(S // tm,),
        in_specs=[row(D_MODEL), row(GMLP_WIDTH), row(DN_WIDTH), row(XA_WIDTH),
                  pl.BlockSpec((D_MODEL, D_MODEL), lambda m: (0, 0)),
                  pl.BlockSpec((1, D_MODEL), lambda m: (0, 0))],
        out_specs=row(D_MODEL),
        compiler_params=pltpu.CompilerParams(dimension_semantics=("arbitrary",),
                                             vmem_limit_bytes=VMEM_LIMIT),
        name="outproj",
    )(x2, oa, ob, oc, w_out, fg)


def _pad_lanes(v):
    return jnp.pad(v.astype(F32), (0, LANES - v.shape[0]))[None, :]


def kernel(x, mem, ln_g, w_in, gmlp_ln_g, gmlp_ln_b, gmlp_ws, gmlp_bs, conv_w, dn_a_log,
           dn_dt_bias, dn_norm_g, mem_norm_g, w_mem_kv, w_out, final_g):
    B, S, _ = x.shape
    assert B == 1 and ln_g.shape[0] == 1 and S % (DN_PAIR * DN_BLOCK) == 0
    x2 = x[0]
    w_t = jnp.swapaxes(w_in, 1, 2)[0]
    w_c = w_t[GATE_COL0 + 2 * DN_HEADS:]
    w_ab = jnp.pad(w_t[GATE_COL0:GATE_COL0 + 2 * DN_HEADS], ((0, LANES - 2 * DN_HEADS), (0, 0)))

    h, ab = _norm(x2, ln_g, w_ab)
    qkv = _qkv(h, w_t, conv_w[0])
    p = _proj(h, w_t, w_c)
    out_a = _gmlp(p, gmlp_ln_g, gmlp_ln_b, gmlp_ws[0], gmlp_bs[0].T)
    kv = _memkv(mem[0], mem_norm_g, w_mem_kv[0].astype(BF16))
    out_c = _xattn(p, kv)
    out_b = _deltanet(qkv, p, ab, _pad_lanes(dn_a_log[0]), _pad_lanes(dn_dt_bias[0]), dn_norm_g)
    y = _outproj(x2, out_a, out_b, out_c, w_out[0].astype(BF16), final_g[None, :])
    return y[None]
```
